```python
import math
import jax, jax.numpy as jnp
from jax import lax
import numpy as np

D_MODEL = 4096
BATCH = 4
SEQ = 4096
DEPTH = 2
DEC_BATCH = 4
DEC_SEQ = 2048
PAST_LEN = 128

N_META = 16
D_FF = 11008
HG_DIM = D_MODEL // 2
HG_HEAD_DIM = 128
HG_HEADS = HG_DIM // HG_HEAD_DIM
HG_CHUNK = 64
MLA_HEADS = D_MODEL // 256
QK_NOPE = 128
QK_ROPE = 64
QK_HEAD = QK_NOPE + QK_ROPE
V_HEAD = 128
Q_LORA = (3 * D_MODEL) // 16
KV_LORA = D_MODEL // 8
ROPE_THETA = 10000.0
Q_BLOCK = 128
RMS_EPS = 1e-6
MIXER_SPLIT_SIZES = (HG_DIM, HG_DIM, HG_DIM, HG_DIM, HG_DIM, Q_LORA, KV_LORA, QK_ROPE, D_MODEL, D_MODEL)
W_IN_COLS = sum(MIXER_SPLIT_SIZES)

kernel_name = "hybrid_hgrn2_mla_macaron_encoder"


def _split_points():
    pts, acc = [], 0
    for s in MIXER_SPLIT_SIZES[:-1]:
        acc += s
        pts.append(acc)
    return pts


def rms_norm(x, w):
    xf = x.astype(jnp.float32)
    y = xf * lax.rsqrt(jnp.mean(xf * xf, axis=-1, keepdims=True) + RMS_EPS)
    return (y * w.astype(jnp.float32)).astype(x.dtype)


def swiglu(x, w_in, w_out):
    gate, up = jnp.split(x @ w_in, 2, axis=-1)
    return (jax.nn.silu(gate) * up) @ w_out


def _rope_tables(length, dtype):
    inv_freq = 1.0 / (ROPE_THETA ** (jnp.arange(0, QK_ROPE, 2, dtype=jnp.float32) / QK_ROPE))
    ang = jnp.arange(length, dtype=jnp.float32)[:, None] * inv_freq[None, :]
    return jnp.cos(ang).astype(dtype), jnp.sin(ang).astype(dtype)


def _rope_tail(x, cos, sin):
    x_nope, x_rope = x[..., :QK_NOPE], x[..., QK_NOPE:]
    r1, r2 = jnp.split(x_rope, 2, axis=-1)
    c = cos[None, :, None, :]
    s = sin[None, :, None, :]
    return jnp.concatenate([x_nope, r1 * c - r2 * s, r2 * c + r1 * s], axis=-1)


def _hgrn_lower_bounds(lb_logits):
    p = jax.nn.softmax(lb_logits.astype(jnp.float32), axis=1)
    c = jnp.cumsum(p, axis=1)
    return c - c[:, :1]


def _hgrn2_gate(z, lb):
    z = z.astype(jnp.float32)
    log_f = jnp.logaddexp(jnp.log(lb), jnp.log1p(-lb) + jax.nn.log_sigmoid(z))
    k = (1.0 - lb) * jax.nn.sigmoid(-z)
    return k, log_f


def _gla_chunk_scan(q, k, v, g):
    B, T, H, dk = q.shape
    dv = v.shape[-1]
    n = T // HG_CHUNK

    def to_chunks(a):
        return a.reshape(B, n, HG_CHUNK, H, a.shape[-1]).transpose(1, 0, 3, 2, 4)

    causal = jnp.tril(jnp.ones((HG_CHUNK, HG_CHUNK), dtype=bool))[None, None, :, :, None]

    def step(S, inp):
        qc, kc, vc, gc = inp
        b = jnp.cumsum(gc, axis=2)
        o_inter = jnp.einsum('bhtk,bhkv->bhtv', qc * jnp.exp(b), S)
        diff = b[:, :, :, None, :] - b[:, :, None, :, :]
        decay = jnp.exp(jnp.where(causal, diff, -jnp.inf))
        A = jnp.einsum('bhtk,bhsk,bhtsk->bhts', qc, kc, decay)
        o = o_inter + jnp.einsum('bhts,bhsv->bhtv', A, vc)
        b_last = b[:, :, -1:, :]
        S = jnp.exp(b_last[:, :, 0, :])[..., None] * S + jnp.einsum('bhsk,bhsv->bhkv', kc * jnp.exp(b_last - b), vc)
        return S, o

    S0 = jnp.zeros((B, H, dk, dv), jnp.float32)
    _, o = lax.scan(step, S0, (to_chunks(q), to_chunks(k), to_chunks(v), to_chunks(g)))
    return o.transpose(1, 0, 3, 2, 4).reshape(B, T, H, dv)


def _hgrn2_bidirectional(zq, zf_fwd, zf_bwd, zi, lb_fwd, lb_bwd):
    B, L, _ = zq.shape
    pad = HG_CHUNK - N_META

    def heads(a):
        a = jnp.pad(a.astype(jnp.float32), ((0, 0), (pad, 0), (0, 0)))
        return a.reshape(B, L + pad, HG_HEADS, HG_HEAD_DIM)

    q = heads(jax.nn.silu(zq.astype(jnp.float32)))
    v = heads(zi)
    k_f, g_f = _hgrn2_gate(zf_fwd, lb_fwd)
    k_b, g_b = _hgrn2_gate(zf_bwd, lb_bwd)
    k_f, g_f, k_b, g_b = heads(k_f), heads(g_f), heads(k_b), heads(g_b)
    o_f = _gla_chunk_scan(q, k_f, v, g_f)
    flip = lambda a: jnp.flip(a, axis=1)
    o_b = flip(_gla_chunk_scan(flip(q), flip(k_b), flip(v), flip(g_b)))
    return (o_f + o_b)[:, pad:]


def _bidirectional_attention(q, k, v):
    B, L, H, dq = q.shape
    nb = -(-L // Q_BLOCK)
    qp = jnp.pad(q, ((0, 0), (0, nb * Q_BLOCK - L), (0, 0), (0, 0)))
    qb = qp.reshape(B, nb, Q_BLOCK, H, dq).transpose(1, 0, 2, 3, 4)
    scale = 1.0 / math.sqrt(dq)

    def attend(qblk):
        s = jnp.einsum('bqhd,bkhd->bhqk', qblk, k).astype(jnp.float32) * scale
        p = jax.nn.softmax(s, axis=-1).astype(v.dtype)
        return jnp.einsum('bhqk,bkhd->bqhd', p, v)

    o = lax.map(attend, qb)
    return o.transpose(1, 0, 2, 3, 4).reshape(B, nb * Q_BLOCK, H, V_HEAD)[:, :L]


def _mla(c_q, c_kv, k_r, q_lat_norm, w_uq, kv_lat_norm, w_ukv, q_head_norm, k_head_norm, cos, sin):
    B, L, _ = c_q.shape
    q = (rms_norm(c_q, q_lat_norm) @ w_uq).reshape(B, L, MLA_HEADS, QK_HEAD)
    kv = (rms_norm(c_kv, kv_lat_norm) @ w_ukv).reshape(B, L, MLA_HEADS, QK_NOPE + V_HEAD)
    k_nope, v = kv[..., :QK_NOPE], kv[..., QK_NOPE:]
    k_rope = jnp.broadcast_to(k_r[:, :, None, :], (B, L, MLA_HEADS, QK_ROPE))
    k = jnp.concatenate([k_nope, k_rope], axis=-1)
    q = _rope_tail(rms_norm(q, q_head_norm), cos, sin)
    k = _rope_tail(rms_norm(k, k_head_norm), cos, sin)
    o = _bidirectional_attention(q, k, v)
    return o.reshape(B, L, MLA_HEADS * V_HEAD)


def _token_mixer(u, lb_fwd, lb_bwd, w_in, q_lat_norm, w_uq, kv_lat_norm, w_ukv, q_head_norm, k_head_norm,
                 hg_out_norm, w_branch_hgrn, w_branch_mla, w_out, cos, sin):
    B, L, _ = u.shape
    proj = u @ w_in
    zq, zf_f, zf_b, zi, zg, c_q, c_kv, k_r, gate_h, gate_a = jnp.split(proj, _split_points(), axis=-1)
    o_h = _hgrn2_bidirectional(zq, zf_f, zf_b, zi, lb_fwd, lb_bwd)
    o_h = rms_norm(o_h, hg_out_norm) * jax.nn.silu(zg.astype(jnp.float32).reshape(B, L, HG_HEADS, HG_HEAD_DIM))
    br_h = o_h.reshape(B, L, HG_DIM).astype(u.dtype) @ w_branch_hgrn
    o_a = _mla(c_q, c_kv, k_r, q_lat_norm, w_uq, kv_lat_norm, w_ukv, q_head_norm, k_head_norm, cos, sin)
    br_a = o_a @ w_branch_mla
    merged = jax.nn.sigmoid(gate_h) * br_h + jax.nn.sigmoid(gate_a) * br_a
    return merged @ w_out


def _encoder_trunk(x, meta_tokens, hgrn_lb_logits, norm_ffn1, w_ffn1_in, w_ffn1_out, norm_mix, w_in,
                   q_lat_norm, w_uq, kv_lat_norm, w_ukv, q_head_norm, k_head_norm, hg_out_norm,
                   w_branch_hgrn, w_branch_mla, w_out, norm_ffn2, w_ffn2_in, w_ffn2_out):
    B, S, _ = x.shape
    meta = jnp.broadcast_to(meta_tokens.astype(x.dtype)[None], (B, N_META, D_MODEL))
    h = jnp.concatenate([meta, x], axis=1)
    L = S + N_META
    cos, sin = _rope_tables(L, x.dtype)
    lb = _hgrn_lower_bounds(hgrn_lb_logits)
    for l in range(DEPTH):
        h = h + 0.5 * swiglu(rms_norm(h, norm_ffn1[l]), w_ffn1_in[l], w_ffn1_out[l])
        h = h + _token_mixer(rms_norm(h, norm_mix[l]), lb[0, l], lb[1, l], w_in[l], q_lat_norm[l], w_uq[l],
                             kv_lat_norm[l], w_ukv[l], q_head_norm[l], k_head_norm[l], hg_out_norm[l],
                             w_branch_hgrn[l], w_branch_mla[l], w_out[l], cos, sin)
        h = h + 0.5 * swiglu(rms_norm(h, norm_ffn2[l]), w_ffn2_in[l], w_ffn2_out[l])
    return h[:, N_META:]


def setup_inputs(seed: int = 0) -> dict:
    key = jax.random.key(seed)
    ks = jax.random.split(key, 24)
    f32 = jnp.float32

    def nrm(k, shape, scale):
        return jax.random.normal(k, shape, f32) * scale

    def gain(k, shape):
        return 1.0 + 0.02 * jax.random.normal(k, shape, f32)

    return {
        "x_prompt": nrm(ks[0], (BATCH, SEQ, D_MODEL), 1.0),
        "x_sample": nrm(ks[1], (DEC_BATCH, DEC_SEQ, D_MODEL), 1.0),
        "meta_tokens": nrm(ks[2], (N_META, D_MODEL), 1.0),
        "hgrn_lb_logits": nrm(ks[3], (2, DEPTH, HG_DIM), 0.5),
        "norm_ffn1": gain(ks[4], (DEPTH, D_MODEL)),
        "w_ffn1_in": nrm(ks[5], (DEPTH, D_MODEL, 2 * D_FF), D_MODEL ** -0.5),
        "w_ffn1_out": nrm(ks[6], (DEPTH, D_FF, D_MODEL), D_FF ** -0.5),
        "norm_mix": gain(ks[7], (DEPTH, D_MODEL)),
        "w_in": nrm(ks[8], (DEPTH, D_MODEL, W_IN_COLS), D_MODEL ** -0.5),
        "q_lat_norm": gain(ks[9], (DEPTH, Q_LORA)),
        "w_uq": nrm(ks[10], (DEPTH, Q_LORA, MLA_HEADS * QK_HEAD), Q_LORA ** -0.5),
        "kv_lat_norm": gain(ks[11], (DEPTH, KV_LORA)),
        "w_ukv": nrm(ks[12], (DEPTH, KV_LORA, MLA_HEADS * (QK_NOPE + V_HEAD)), KV_LORA ** -0.5),
        "q_head_norm": gain(ks[13], (DEPTH, QK_HEAD)),
        "k_head_norm": gain(ks[14], (DEPTH, QK_HEAD)),
        "hg_out_norm": gain(ks[15], (DEPTH, HG_HEAD_DIM)),
        "w_branch_hgrn": nrm(ks[16], (DEPTH, HG_DIM, D_MODEL), HG_DIM ** -0.5),
        "w_branch_mla": nrm(ks[17], (DEPTH, MLA_HEADS * V_HEAD, D_MODEL), (MLA_HEADS * V_HEAD) ** -0.5),
        "w_out": nrm(ks[18], (DEPTH, D_MODEL, D_MODEL), D_MODEL ** -0.5),
        "norm_ffn2": gain(ks[19], (DEPTH, D_MODEL)),
        "w_ffn2_in": nrm(ks[20], (DEPTH, D_MODEL, 2 * D_FF), D_MODEL ** -0.5),
        "w_ffn2_out": nrm(ks[21], (DEPTH, D_FF, D_MODEL), D_FF ** -0.5),
    }


def reference(x_prompt, x_sample, meta_tokens, hgrn_lb_logits, norm_ffn1, w_ffn1_in, w_ffn1_out, norm_mix, w_in,
              q_lat_norm, w_uq, kv_lat_norm, w_ukv, q_head_norm, k_head_norm, hg_out_norm,
              w_branch_hgrn, w_branch_mla, w_out, norm_ffn2, w_ffn2_in, w_ffn2_out):
    y_prompt = _encoder_trunk(x_prompt, meta_tokens, hgrn_lb_logits, norm_ffn1, w_ffn1_in, w_ffn1_out, norm_mix,
                              w_in, q_lat_norm, w_uq, kv_lat_norm, w_ukv, q_head_norm, k_head_norm, hg_out_norm,
                              w_branch_hgrn, w_branch_mla, w_out, norm_ffn2, w_ffn2_in, w_ffn2_out)
    y_sample = _encoder_trunk(x_sample, meta_tokens, hgrn_lb_logits, norm_ffn1, w_ffn1_in, w_ffn1_out, norm_mix,
                              w_in, q_lat_norm, w_uq, kv_lat_norm, w_ukv, q_head_norm, k_head_norm, hg_out_norm,
                              w_branch_hgrn, w_branch_mla, w_out, norm_ffn2, w_ffn2_in, w_ffn2_out)
    return (y_prompt, y_sample)
```

```python
import functools
import math

import jax
import jax.numpy as jnp
from jax import lax
from jax.experimental import pallas as pl
from jax.experimental.pallas import tpu as pltpu

N_META = 16
HG_HEAD_DIM = 128
QK_NOPE = 128
QK_ROPE = 64
QK_HEAD = QK_NOPE + QK_ROPE
V_HEAD = 128
ROPE_THETA = 10000.0
RMS_EPS = 1e-6

LANES = 128
SEQ_PAD = 128
HG_CHUNK_ROWS = 128
HG_SUB_ROWS = 16
ATTN_MAX_Q_ROWS = 640
QK_PAD = 256
VMEM_LIMIT = 56 * 1024 * 1024

F32 = jnp.float32
BF16 = jnp.bfloat16
NEG_BIG = -1e30


def _cparams(sem):
    return pltpu.CompilerParams(dimension_semantics=sem, vmem_limit_bytes=VMEM_LIMIT)


def _pick(n, prefs):
    for p in prefs:
        if n % p == 0:
            return p
    return n


def _sigmoid(x):
    return 1.0 / (1.0 + jnp.exp(-x))


def _rmsnorm_kernel(x_ref, w_ref, o_ref):
    x = x_ref[...]
    ms = jnp.mean(x * x, axis=-1, keepdims=True)
    o_ref[...] = ((x * lax.rsqrt(ms + RMS_EPS)) * w_ref[...]).astype(o_ref.dtype)


def _rmsnorm(x, w):
    m, d = x.shape
    tm = _pick(m, (512, 256, 128, 64, 32, 16, 8))
    return pl.pallas_call(
        _rmsnorm_kernel,
        grid=(m // tm,),
        in_specs=[pl.BlockSpec((tm, d), lambda i: (i, 0)),
                  pl.BlockSpec((1, d), lambda i: (0, 0))],
        out_specs=pl.BlockSpec((tm, d), lambda i: (i, 0)),
        out_shape=jax.ShapeDtypeStruct((m, d), BF16),
        compiler_params=_cparams(("parallel",)),
        name="rmsnorm",
    )(x, w.reshape(1, d).astype(F32))


def _mm_kernel(a_ref, b_ref, o_ref):
    o_ref[...] = jnp.dot(a_ref[...], b_ref[...], preferred_element_type=F32)


def _mm_headmajor_kernel(a_ref, b_ref, o_ref):
    acc = jnp.dot(a_ref[...], b_ref[...], preferred_element_type=F32)
    for j in range(o_ref.shape[0]):
        o_ref[j] = acc[:, j * LANES:(j + 1) * LANES]


def _matmul(a, b, *, head_major=False):
    m, k = a.shape
    n = b.shape[1]
    tm = _pick(m, (1024, 512, 256, 128, 64, 32, 16))
    tn = _pick(n, (512, 256, 128))
    grid = (m // tm, n // tn)
    in_specs = [pl.BlockSpec((tm, k), lambda i, j: (i, 0)),
                pl.BlockSpec((k, tn), lambda i, j: (0, j))]
    if head_major:
        return pl.pallas_call(
            _mm_headmajor_kernel, grid=grid, in_specs=in_specs,
            out_specs=pl.BlockSpec((tn // LANES, tm, LANES), lambda i, j: (j, i, 0)),
            out_shape=jax.ShapeDtypeStruct((n // LANES, m, LANES), F32),
            compiler_params=_cparams(("parallel", "arbitrary")),
            name="proj_headmajor",
        )(a, b)
    return pl.pallas_call(
        _mm_kernel, grid=grid, in_specs=in_specs,
        out_specs=pl.BlockSpec((tm, tn), lambda i, j: (i, j)),
        out_shape=jax.ShapeDtypeStruct((m, n), F32),
        compiler_params=_cparams(("parallel", "arbitrary")),
        name="proj",
    )(a, b)


def _ffn_in_kernel(x_ref, wg_ref, wu_ref, o_ref):
    x = x_ref[...]
    g = jnp.dot(x, wg_ref[...], preferred_element_type=F32)
    u = jnp.dot(x, wu_ref[...], preferred_element_type=F32)
    o_ref[...] = ((g * _sigmoid(g)) * u).astype(o_ref.dtype)


def _ffn_in(x, wg, wu):
    m, k = x.shape
    f = wg.shape[1]
    tm = _pick(m, (1024, 512, 256, 128, 64, 32, 16))
    tf = _pick(f, (512, 256, 128))
    return pl.pallas_call(
        _ffn_in_kernel,
        grid=(m // tm, f // tf),
        in_specs=[pl.BlockSpec((tm, k), lambda i, j: (i, 0)),
                  pl.BlockSpec((k, tf), lambda i, j: (0, j)),
                  pl.BlockSpec((k, tf), lambda i, j: (0, j))],
        out_specs=pl.BlockSpec((tm, tf), lambda i, j: (i, j)),
        out_shape=jax.ShapeDtypeStruct((m, f), BF16),
        compiler_params=_cparams(("parallel", "arbitrary")),
        name="ffn_in",
    )(x, wg, wu)


def _mm_res_kernel(a_ref, b_ref, r_ref, o_ref, acc_ref, *, scale):
    kk = pl.program_id(2)

    @pl.when(kk == 0)
    def _():
        acc_ref[...] = jnp.zeros_like(acc_ref)

    acc_ref[...] += jnp.dot(a_ref[...], b_ref[...], preferred_element_type=F32)

    @pl.when(kk == pl.num_programs(2) - 1)
    def _():
        o_ref[...] = r_ref[...] + scale * acc_ref[...]


def _matmul_residual(a, b, res, scale):
    m, k = a.shape
    n = b.shape[1]
    tm = _pick(m, (1024, 512, 256, 128, 64, 32, 16))
    tn = _pick(n, (1024, 512, 256, 128))
    tk = _pick(k, (1024, 512, 256, 128))
    return pl.pallas_call(
        functools.partial(_mm_res_kernel, scale=scale),
        grid=(m // tm, n // tn, k // tk),
        in_specs=[pl.BlockSpec((tm, tk), lambda i, j, kk: (i, kk)),
                  pl.BlockSpec((tk, tn), lambda i, j, kk: (kk, j)),
                  pl.BlockSpec((tm, tn), lambda i, j, kk: (i, j))],
        out_specs=pl.BlockSpec((tm, tn), lambda i, j, kk: (i, j)),
        out_shape=jax.ShapeDtypeStruct((m, n), F32),
        scratch_shapes=[pltpu.VMEM((tm, tn), F32)],
        compiler_params=_cparams(("parallel", "parallel", "arbitrary")),
        name="proj_residual",
    )(a, b, res)


def _merge_kernel(u_ref, oh_ref, oa_ref, wgh_ref, wga_ref, wbh_ref, wba_ref, o_ref):
    u = u_ref[...]
    oh = jnp.concatenate([oh_ref[h] for h in range(oh_ref.shape[0])], axis=1)
    gh = jnp.dot(u, wgh_ref[...], preferred_element_type=F32)
    ga = jnp.dot(u, wga_ref[...], preferred_element_type=F32)
    bh = jnp.dot(oh, wbh_ref[...], preferred_element_type=F32)
    ba = jnp.dot(oa_ref[...], wba_ref[...], preferred_element_type=F32)
    o_ref[...] = (_sigmoid(gh) * bh + _sigmoid(ga) * ba).astype(o_ref.dtype)


def _merge(u, oh, oa, wgh, wga, wbh, wba):
    m, d = u.shape
    nh = oh.shape[0]
    n = wgh.shape[1]
    tm = _pick(m, (512, 256, 128, 64, 32, 16))
    tn = _pick(n, (256, 128))
    return pl.pallas_call(
        _merge_kernel,
        grid=(m // tm, n // tn),
        in_specs=[pl.BlockSpec((tm, d), lambda i, j: (i, 0)),
                  pl.BlockSpec((nh, tm, LANES), lambda i, j: (0, i, 0)),
                  pl.BlockSpec((tm, oa.shape[1]), lambda i, j: (i, 0)),
                  pl.BlockSpec((d, tn), lambda i, j: (0, j)),
                  pl.BlockSpec((d, tn), lambda i, j: (0, j)),
                  pl.BlockSpec((wbh.shape[0], tn), lambda i, j: (0, j)),
                  pl.BlockSpec((wba.shape[0], tn), lambda i, j: (0, j))],
        out_specs=pl.BlockSpec((tm, tn), lambda i, j: (i, j)),
        out_shape=jax.ShapeDtypeStruct((m, n), BF16),
        compiler_params=_cparams(("parallel", "arbitrary")),
        name="branch_merge",
    )(u, oh, oa, wgh, wga, wbh, wba)


def _rope_tail(y2, tab):
    w = y2 * tab
    rot = w + pltpu.roll(w, QK_ROPE, axis=1)
    lane = lax.broadcasted_iota(jnp.int32, rot.shape, 1)
    return jnp.where(lane < QK_ROPE, rot, 0.0)


def _head_sumsq(y1, y2):
    lane = lax.broadcasted_iota(jnp.int32, y2.shape, 1)
    y2m = jnp.where(lane < QK_ROPE, y2, 0.0)
    return jnp.sum(y1 * y1, axis=-1, keepdims=True) + jnp.sum(y2m * y2m, axis=-1, keepdims=True)


def _qproj_kernel(lat_ref, lw_ref, w_ref, g_ref, tab_ref, q_ref, xn_ref, *, q_lora, scale):
    @pl.when(pl.program_id(1) == 0)
    def _():
        x = lat_ref[:, 0:q_lora]
        ms = jnp.mean(x * x, axis=-1, keepdims=True)
        xn_ref[...] = ((x * lax.rsqrt(ms + RMS_EPS)) * lw_ref[...]).astype(BF16)

    y = jnp.dot(xn_ref[...], w_ref[0], preferred_element_type=F32)
    y1, y2 = y[:, :QK_NOPE], y[:, QK_NOPE:]
    r = lax.rsqrt(_head_sumsq(y1, y2) * (1.0 / QK_HEAD) + RMS_EPS)
    g = g_ref[...]
    q_ref[0, :, 0:QK_NOPE] = (((y1 * r) * g[:, :QK_NOPE]) * scale).astype(BF16)
    q_ref[0, :, QK_NOPE:] = (_rope_tail((y2 * r) * g[:, QK_NOPE:], tab_ref[...]) * scale).astype(BF16)


def _kvproj_kernel(lat_ref, lw_ref, w_ref, g_ref, tab_ref, k_ref, v_ref, xn_ref, *, q_lora, kv_lora):
    @pl.when(pl.program_id(1) == 0)
    def _():
        x = lat_ref[:, q_lora:q_lora + kv_lora]
        ms = jnp.mean(x * x, axis=-1, keepdims=True)
        xn_ref[...] = ((x * lax.rsqrt(ms + RMS_EPS)) * lw_ref[...]).astype(BF16)

    y = jnp.dot(xn_ref[...], w_ref[0], preferred_element_type=F32)
    y1 = y[:, :QK_NOPE]
    y2 = lat_ref[:, q_lora + kv_lora:q_lora + kv_lora + LANES]
    r = lax.rsqrt(_head_sumsq(y1, y2) * (1.0 / QK_HEAD) + RMS_EPS)
    g = g_ref[...]
    k_ref[0, :, 0:QK_NOPE] = ((y1 * r) * g[:, :QK_NOPE]).astype(BF16)
    k_ref[0, :, QK_NOPE:] = _rope_tail((y2 * r) * g[:, QK_NOPE:], tab_ref[...]).astype(BF16)
    v_ref[0] = y[:, QK_NOPE:].astype(BF16)


def _row_tile(rows, row_off):
    return _pick(math.gcd(rows, row_off) if row_off else rows, (512, 256, 128, 64, 32, 16))


def _qproj(lat, lat_w, w_heads, gain, tab, row_off, rows, q_lora):
    nh = w_heads.shape[0]
    tl = _row_tile(rows, row_off)
    off = row_off // tl
    return pl.pallas_call(
        functools.partial(_qproj_kernel, q_lora=q_lora, scale=1.0 / math.sqrt(QK_HEAD)),
        grid=(rows // tl, nh),
        in_specs=[pl.BlockSpec((tl, lat.shape[1]), lambda i, h: (off + i, 0)),
                  pl.BlockSpec((1, q_lora), lambda i, h: (0, 0)),
                  pl.BlockSpec((1, q_lora, QK_PAD), lambda i, h: (h, 0, 0)),
                  pl.BlockSpec((1, QK_PAD), lambda i, h: (0, 0)),
                  pl.BlockSpec((tl, LANES), lambda i, h: (i, 0))],
        out_specs=pl.BlockSpec((1, tl, QK_PAD), lambda i, h: (h, i, 0)),
        out_shape=jax.ShapeDtypeStruct((nh, rows, QK_PAD), BF16),
        scratch_shapes=[pltpu.VMEM((tl, q_lora), BF16)],
        compiler_params=_cparams(("parallel", "arbitrary")),
        name="mla_q_proj",
    )(lat, lat_w, w_heads, gain, tab)


def _kvproj(lat, lat_w, w_heads, gain, tab, row_off, rows, q_lora, kv_lora):
    nh = w_heads.shape[0]
    tl = _row_tile(rows, row_off)
    off = row_off // tl
    return pl.pallas_call(
        functools.partial(_kvproj_kernel, q_lora=q_lora, kv_lora=kv_lora),
        grid=(rows // tl, nh),
        in_specs=[pl.BlockSpec((tl, lat.shape[1]), lambda i, h: (off + i, 0)),
                  pl.BlockSpec((1, kv_lora), lambda i, h: (0, 0)),
                  pl.BlockSpec((1, kv_lora, QK_NOPE + V_HEAD), lambda i, h: (h, 0, 0)),
                  pl.BlockSpec((1, QK_PAD), lambda i, h: (0, 0)),
                  pl.BlockSpec((tl, LANES), lambda i, h: (i, 0))],
        out_specs=[pl.BlockSpec((1, tl, QK_PAD), lambda i, h: (h, i, 0)),
                   pl.BlockSpec((1, tl, V_HEAD), lambda i, h: (h, i, 0))],
        out_shape=[jax.ShapeDtypeStruct((nh, rows, QK_PAD), BF16),
                   jax.ShapeDtypeStruct((nh, rows, V_HEAD), BF16)],
        scratch_shapes=[pltpu.VMEM((tl, kv_lora), BF16)],
        compiler_params=_cparams(("parallel", "arbitrary")),
        name="mla_kv_proj",
    )(lat, lat_w, w_heads, gain, tab)


def _attn_kernel(q_ref, k_ref, v_ref, o_ref, *, n_valid):
    s = lax.dot_general(q_ref[0], k_ref[0], (((1,), (1,)), ((), ())), preferred_element_type=F32)
    col = lax.broadcasted_iota(jnp.int32, s.shape, 1)
    s = jnp.where(col < n_valid, s, NEG_BIG)
    p = jnp.exp(s - jnp.max(s, axis=-1, keepdims=True))
    denom = jnp.sum(p, axis=-1, keepdims=True)
    o = jnp.dot(p.astype(BF16), v_ref[0], preferred_element_type=F32)
    o_ref[...] = (o / denom).astype(o_ref.dtype)


def _attention(q, k, v, batch, lp, n_valid):
    nh = q.shape[0]
    nq = next(n for n in range(1, lp + 1) if lp % n == 0 and (lp // n) % 16 == 0 and lp // n <= ATTN_MAX_Q_ROWS)
    tq = lp // nq
    return pl.pallas_call(
        functools.partial(_attn_kernel, n_valid=n_valid),
        grid=(batch, nh, nq),
        in_specs=[pl.BlockSpec((1, tq, QK_PAD), lambda b, h, i: (h, b * nq + i, 0)),
                  pl.BlockSpec((1, lp, QK_PAD), lambda b, h, i: (h, b, 0)),
                  pl.BlockSpec((1, lp, V_HEAD), lambda b, h, i: (h, b, 0))],
        out_specs=pl.BlockSpec((tq, V_HEAD), lambda b, h, i: (b * nq + i, h)),
        out_shape=jax.ShapeDtypeStruct((batch * lp, nh * V_HEAD), BF16),
        compiler_params=_cparams(("parallel", "parallel", "arbitrary")),
        name="mla_attention",
    )(q, k, v)


def _log_sigmoid(z):
    return jnp.minimum(z, 0.0) - jnp.log1p(jnp.exp(-jnp.abs(z)))


def _lower_bound(logit_ref, h, layer):
    lg = logit_ref[0, h]
    e = jnp.exp(lg - jnp.max(lg, axis=0, keepdims=True))
    p = e / jnp.sum(e, axis=0, keepdims=True)
    lb = jnp.zeros((1, lg.shape[1]), F32)
    for j in range(1, layer + 1):
        lb = lb + p[j:j + 1]
    return lb


def _hgrn_chunk(zq, zf, zi, lb, row_ok, st, tri, ones_sq, reverse):
    c_rows = zq.shape[0]
    sub = HG_SUB_ROWS
    nb = c_rows // sub
    q = zq * _sigmoid(zq)
    v = zi
    k = jnp.where(row_ok, (1.0 - lb) * _sigmoid(-zf), 0.0)
    la = jnp.log(lb)
    lbv = jnp.log1p(-lb) + _log_sigmoid(zf)
    g = jnp.maximum(la, lbv) + jnp.log1p(jnp.exp(-jnp.abs(la - lbv)))

    g1 = g.astype(BF16)
    r1 = g - g1.astype(F32)
    g2 = r1.astype(BF16)
    g3 = (r1 - g2.astype(F32)).astype(BF16)
    bb = jnp.dot(tri, jnp.concatenate([g1, g2, g3], axis=1), preferred_element_type=F32)
    b = (bb[:, :LANES] + bb[:, LANES:2 * LANES]) + bb[:, 2 * LANES:]
    total = b[0:1] if reverse else b[c_rows - 1:c_rows]

    qh = (q * jnp.exp(b)).astype(BF16)
    kh = (k * jnp.exp(total - b)).astype(BF16)
    o = lax.dot_general(qh, st.astype(BF16), (((1,), (1,)), ((), ())), preferred_element_type=F32)
    st_new = st * jnp.exp(total) + lax.dot_general(
        v.astype(BF16), kh, (((0,), (0,)), ((), ())), preferred_element_type=F32)

    vb = v.astype(BF16)
    pieces = []
    for blk in range(nb):
        lo, hi = blk * sub, (blk + 1) * sub
        if reverse:
            if blk == nb - 1:
                pieces.append(jnp.zeros((sub, LANES), F32))
                continue
            ref = b[hi:hi + 1]
            other = slice(hi, c_rows)
        else:
            if blk == 0:
                pieces.append(jnp.zeros((sub, LANES), F32))
                continue
            ref = b[lo - 1:lo]
            other = slice(0, lo)
        qi = (q[lo:hi] * jnp.exp(b[lo:hi] - ref)).astype(BF16)
        ko = (k[other] * jnp.exp(ref - b[other])).astype(BF16)
        a = lax.dot_general(qi, ko, (((1,), (1,)), ((), ())), preferred_element_type=F32)
        pieces.append(jnp.dot(a.astype(BF16), vb[other], preferred_element_type=F32))
    o = o + jnp.concatenate(pieces, axis=0)

    pos = lax.broadcasted_iota(jnp.int32, (c_rows, LANES), 0) % sub
    d_list, v_list = [], []
    for d in range(sub):
        if d == 0:
            ks, bs, vs = k, b, v
            dmat = q * ks
        else:
            shift = (c_rows - d) if reverse else d
            ks = pltpu.roll(k, shift, axis=0)
            bs = pltpu.roll(b, shift, axis=0)
            vs = pltpu.roll(v, shift, axis=0)
            ok = (pos + d < sub) if reverse else (pos >= d)
            dmat = (q * ks) * jnp.exp(jnp.where(ok, b - bs, NEG_BIG))
        d_list.append(dmat.astype(BF16))
        v_list.append(vs)
    rs = jnp.dot(jnp.concatenate(d_list, axis=0), ones_sq, preferred_element_type=F32)
    for d in range(sub):
        o = o + rs[d * c_rows:(d + 1) * c_rows] * v_list[d]
    return o, st_new


def _hgrn_consts(c_rows, reverse):
    r = lax.broadcasted_iota(jnp.int32, (c_rows, c_rows), 0)
    c = lax.broadcasted_iota(jnp.int32, (c_rows, c_rows), 1)
    tri = jnp.where((c >= r) if reverse else (c <= r), 1.0, 0.0).astype(BF16)
    return tri, jnp.ones((LANES, LANES), BF16)


def _hgrn_fwd_kernel(info_ref, zq_ref, zf_ref, zi_ref, lg_ref, o_ref, st_ref, *, layer):
    g = pl.program_id(0)
    local, seq_rows = info_ref[3 * g], info_ref[3 * g + 2]
    nh, c_rows = zq_ref.shape[0], zq_ref.shape[1]

    @pl.when(local == 0)
    def _():
        st_ref[...] = jnp.zeros_like(st_ref)

    tri, ones_sq = _hgrn_consts(c_rows, False)
    row_ok = (lax.broadcasted_iota(jnp.int32, (c_rows, LANES), 0) + local * c_rows) < seq_rows

    def body(h, carry):
        lb = _lower_bound(lg_ref, h, layer)
        o, st_new = _hgrn_chunk(zq_ref[h], zf_ref[h], zi_ref[h], lb, row_ok, st_ref[h], tri, ones_sq, False)
        st_ref[h] = st_new
        o_ref[h] = o
        return carry

    lax.fori_loop(0, nh, body, 0)


def _hgrn_bwd_kernel(info_ref, zq_ref, zf_ref, zi_ref, zg_ref, of_ref, lg_ref, nw_ref, o_ref, st_ref, *, layer):
    g = pl.num_programs(0) - 1 - pl.program_id(0)
    local, n_chunks, seq_rows = info_ref[3 * g], info_ref[3 * g + 1], info_ref[3 * g + 2]
    nh, c_rows = zq_ref.shape[0], zq_ref.shape[1]

    @pl.when(local == n_chunks - 1)
    def _():
        st_ref[...] = jnp.zeros_like(st_ref)

    tri, ones_sq = _hgrn_consts(c_rows, True)
    row_ok = (lax.broadcasted_iota(jnp.int32, (c_rows, LANES), 0) + local * c_rows) < seq_rows
    nw = nw_ref[...]

    def body(h, carry):
        lb = _lower_bound(lg_ref, h, layer)
        o, st_new = _hgrn_chunk(zq_ref[h], zf_ref[h], zi_ref[h], lb, row_ok, st_ref[h], tri, ones_sq, True)
        st_ref[h] = st_new
        o = o + of_ref[h]
        ms = jnp.mean(o * o, axis=-1, keepdims=True)
        y = (o * lax.rsqrt(ms + RMS_EPS)) * nw
        zg = zg_ref[h]
        o_ref[h] = (y * (zg * _sigmoid(zg))).astype(o_ref.dtype)
        return carry

    lax.fori_loop(0, nh, body, 0)


def _hgrn(hg, info, lb_logits, out_norm, layer, nh):
    m = hg.shape[1]
    c = HG_CHUNK_ROWS
    n_chunks = m // c
    depth = lb_logits.shape[1]
    lg = lb_logits.astype(F32).reshape(2, depth, nh, LANES).transpose(0, 2, 1, 3)

    def blk(grp, rev):
        if rev:
            return pl.BlockSpec((nh, c, LANES), lambda g, info_ref: (grp, n_chunks - 1 - g, 0))
        return pl.BlockSpec((nh, c, LANES), lambda g, info_ref: (grp, g, 0))

    o_f = pl.pallas_call(
        functools.partial(_hgrn_fwd_kernel, layer=layer),
        grid_spec=pltpu.PrefetchScalarGridSpec(
            num_scalar_prefetch=1, grid=(n_chunks,),
            in_specs=[blk(0, False), blk(1, False), blk(3, False),
                      pl.BlockSpec((1, nh, depth, LANES), lambda g, info_ref: (0, 0, 0, 0))],
            out_specs=blk(0, False),
            scratch_shapes=[pltpu.VMEM((nh, LANES, LANES), F32)]),
        out_shape=jax.ShapeDtypeStruct((nh, m, LANES), F32),
        compiler_params=_cparams(("arbitrary",)),
        name="hgrn_forward",
    )(info, hg, hg, hg, lg)

    return pl.pallas_call(
        functools.partial(_hgrn_bwd_kernel, layer=layer),
        grid_spec=pltpu.PrefetchScalarGridSpec(
            num_scalar_prefetch=1, grid=(n_chunks,),
            in_specs=[blk(0, True), blk(2, True), blk(3, True), blk(4, True), blk(0, True),
                      pl.BlockSpec((1, nh, depth, LANES), lambda g, info_ref: (1, 0, 0, 0)),
                      pl.BlockSpec((1, LANES), lambda g, info_ref: (0, 0))],
            out_specs=blk(0, True),
            scratch_shapes=[pltpu.VMEM((nh, LANES, LANES), F32)]),
        out_shape=jax.ShapeDtypeStruct((nh, m, LANES), BF16),
        compiler_params=_cparams(("arbitrary",)),
        name="hgrn_backward",
    )(info, hg, hg, hg, hg, o_f, lg, out_norm.reshape(1, LANES).astype(F32))


def _swap_halves(x, axis):
    a, b = jnp.split(x, 2, axis=axis)
    return jnp.concatenate([b, a], axis=axis)


def _rope_table(rows):
    inv_freq = 1.0 / (ROPE_THETA ** (jnp.arange(0, QK_ROPE, 2, dtype=F32) / QK_ROPE))
    ang = jnp.arange(rows, dtype=F32)[:, None] * inv_freq[None, :]
    c, s = jnp.cos(ang), jnp.sin(ang)
    return jnp.concatenate([c, c, -s, s], axis=1)


def _head_gain(g):
    g = g.astype(F32)
    return jnp.concatenate([g[:QK_NOPE], g[QK_NOPE:], _swap_halves(g[QK_NOPE:], 0)]).reshape(1, QK_PAD)


def _ffn_weights(w_in, w_out):
    d_ff = w_out.shape[0]
    f_pad = -(-d_ff // 1024) * 1024 if d_ff >= 1024 else -(-d_ff // LANES) * LANES
    pad = f_pad - d_ff
    wg = jnp.pad(w_in[:, :d_ff].astype(BF16), ((0, 0), (0, pad)))
    wu = jnp.pad(w_in[:, d_ff:].astype(BF16), ((0, 0), (0, pad)))
    wo = jnp.pad(w_out.astype(BF16), ((0, pad), (0, 0)))
    return wg, wu, wo


def _ffn(h, norm_w, weights, scale=0.5):
    wg, wu, wo = weights
    return _matmul_residual(_ffn_in(_rmsnorm(h, norm_w), wg, wu), wo, h, scale)


def kernel(x_prompt, x_sample, meta_tokens, hgrn_lb_logits, norm_ffn1, w_ffn1_in, w_ffn1_out, norm_mix, w_in,
           q_lat_norm, w_uq, kv_lat_norm, w_ukv, q_head_norm, k_head_norm, hg_out_norm,
           w_branch_hgrn, w_branch_mla, w_out, norm_ffn2, w_ffn2_in, w_ffn2_out):
    depth = norm_ffn1.shape[0]
    d_model = x_prompt.shape[-1]
    hg_dim = hgrn_lb_logits.shape[-1]
    hg_heads = hg_dim // HG_HEAD_DIM
    q_lora = q_lat_norm.shape[-1]
    kv_lora = kv_lat_norm.shape[-1]
    mla_heads = w_uq.shape[-1] // QK_HEAD

    trunks = []
    row_off = 0
    blocks, info = [], []
    for x in (x_prompt, x_sample):
        bsz, s, _ = x.shape
        lp = s + SEQ_PAD
        meta = jnp.broadcast_to(meta_tokens.astype(x.dtype)[None], (bsz, N_META, d_model))
        pad = jnp.zeros((bsz, lp - s - N_META, d_model), x.dtype)
        blocks.append(jnp.concatenate([meta, x, pad], axis=1).reshape(bsz * lp, d_model))
        nc = lp // HG_CHUNK_ROWS
        for _ in range(bsz):
            for c in range(nc):
                info += [c, nc, s + N_META]
        trunks.append((bsz, s, lp, row_off))
        row_off += bsz * lp
    h = jnp.concatenate(blocks, axis=0).astype(F32)
    info = jnp.asarray(info, jnp.int32)
    tabs = [jnp.tile(_rope_table(lp), (bsz, 1)) for (bsz, s, lp, _) in trunks]

    sizes = (hg_dim,) * 5 + (q_lora, kv_lora, QK_ROPE, d_model, d_model)
    offs = [0]
    for sz in sizes:
        offs.append(offs[-1] + sz)

    for l in range(depth):
        ffn1 = _ffn_weights(w_ffn1_in[l], w_ffn1_out[l])
        ffn2 = _ffn_weights(w_ffn2_in[l], w_ffn2_out[l])
        wl = w_in[l]
        w_hg = wl[:, :offs[5]].astype(BF16)
        w_kr = wl[:, offs[7]:offs[8]]
        w_lat = jnp.concatenate([wl[:, offs[5]:offs[7]], w_kr, _swap_halves(w_kr, 1)], axis=1).astype(BF16)
        w_gh = wl[:, offs[8]:offs[9]].astype(BF16)
        w_ga = wl[:, offs[9]:offs[10]].astype(BF16)
        wq = w_uq[l].reshape(q_lora, mla_heads, QK_HEAD)
        wq = jnp.concatenate([wq, _swap_halves(wq[:, :, QK_NOPE:], 2)], axis=2)
        wq = wq.transpose(1, 0, 2).astype(BF16)
        wkv = w_ukv[l].reshape(kv_lora, mla_heads, QK_NOPE + V_HEAD).transpose(1, 0, 2).astype(BF16)

        h = _ffn(h, norm_ffn1[l], ffn1)

        u = _rmsnorm(h, norm_mix[l])
        hg = _matmul(u, w_hg, head_major=True)
        lat = _matmul(u, w_lat)
        o_h = _hgrn(hg, info, hgrn_lb_logits, hg_out_norm[l], l, hg_heads)

        o_a = []
        for (bsz, s, lp, off), tab in zip(trunks, tabs):
            rows = bsz * lp
            q = _qproj(lat, q_lat_norm[l].reshape(1, q_lora).astype(F32), wq, _head_gain(q_head_norm[l]),
                       tab, off, rows, q_lora)
            k, v = _kvproj(lat, kv_lat_norm[l].reshape(1, kv_lora).astype(F32), wkv, _head_gain(k_head_norm[l]),
                           tab, off, rows, q_lora, kv_lora)
            o_a.append(_attention(q, k, v, bsz, lp, s + N_META))
        o_a = jnp.concatenate(o_a, axis=0)

        merged = _merge(u, o_h, o_a, w_gh, w_ga, w_branch_hgrn[l].astype(BF16), w_branch_mla[l].astype(BF16))
        h = _matmul_residual(merged, w_out[l].astype(BF16), h, 1.0)

        h = _ffn(h, norm_ffn2[l], ffn2)

    outs = []
    for (bsz, s, lp, off) in trunks:
        outs.append(h[off:off + bsz * lp].reshape(bsz, lp, d_model)[:, N_META:N_META + s])
    return tuple(outs)
```

```python
import functools
import math

import jax
import jax.numpy as jnp
from jax import lax
from jax.experimental import pallas as pl
from jax.experimental.pallas import tpu as pltpu

N_META = 16
HG_HEAD_DIM = 128
QK_NOPE = 128
QK_ROPE = 64
QK_HEAD = QK_NOPE + QK_ROPE
V_HEAD = 128
ROPE_THETA = 10000.0
RMS_EPS = 1e-6

LANES = 128
SEQ_PAD = 128
HG_CHUNK_ROWS = 128
HG_SUB_ROWS = 32
HG_EXACT_SUB_ROWS = 16
HG_MAX_LOG_GROWTH = 80.0
ATTN_MAX_Q_ROWS = 640
QK_PAD = 256
VMEM_LIMIT = 56 * 1024 * 1024

F32 = jnp.float32
BF16 = jnp.bfloat16
NEG_BIG = -1e30
LN2 = math.log(2.0)
LOG2E = 1.0 / LN2


def _cparams(sem):
    return pltpu.CompilerParams(dimension_semantics=sem, vmem_limit_bytes=VMEM_LIMIT)


def _pick(n, prefs):
    for p in prefs:
        if n % p == 0:
            return p
    return n


def _sigmoid(x):
    return 1.0 / (1.0 + jnp.exp(-x))


def _rmsnorm_kernel(x_ref, w_ref, o_ref):
    x = x_ref[...]
    ms = jnp.mean(x * x, axis=-1, keepdims=True)
    o_ref[...] = ((x * lax.rsqrt(ms + RMS_EPS)) * w_ref[...]).astype(o_ref.dtype)


def _rmsnorm(x, w):
    m, d = x.shape
    tm = _pick(m, (512, 256, 128, 64, 32, 16, 8))
    return pl.pallas_call(
        _rmsnorm_kernel,
        grid=(m // tm,),
        in_specs=[pl.BlockSpec((tm, d), lambda i: (i, 0)),
                  pl.BlockSpec((1, d), lambda i: (0, 0))],
        out_specs=pl.BlockSpec((tm, d), lambda i: (i, 0)),
        out_shape=jax.ShapeDtypeStruct((m, d), BF16),
        compiler_params=_cparams(("parallel",)),
        name="rmsnorm",
    )(x, w.reshape(1, d).astype(F32))


def _mm_kernel(a_ref, b_ref, o_ref):
    o_ref[...] = jnp.dot(a_ref[...], b_ref[...], preferred_element_type=F32)


def _mm_headmajor_kernel(a_ref, b_ref, o_ref):
    acc = jnp.dot(a_ref[...], b_ref[...], preferred_element_type=F32)
    for j in range(o_ref.shape[0]):
        o_ref[j] = acc[:, j * LANES:(j + 1) * LANES]


def _matmul(a, b, *, head_major=False):
    m, k = a.shape
    n = b.shape[1]
    if n % 256 == 0:
        tm = _pick(m, (1024, 512, 256, 128, 64, 32, 16))
        tn = _pick(n, (512, 256))
    else:
        tm = _pick(m, (512, 256, 128, 64, 32, 16))
        tn = n
    grid = (m // tm, n // tn)
    in_specs = [pl.BlockSpec((tm, k), lambda i, j: (i, 0)),
                pl.BlockSpec((k, tn), lambda i, j: (0, j))]
    if head_major:
        return pl.pallas_call(
            _mm_headmajor_kernel, grid=grid, in_specs=in_specs,
            out_specs=pl.BlockSpec((tn // LANES, tm, LANES), lambda i, j: (j, i, 0)),
            out_shape=jax.ShapeDtypeStruct((n // LANES, m, LANES), F32),
            compiler_params=_cparams(("parallel", "arbitrary")),
            name="proj_headmajor",
        )(a, b)
    return pl.pallas_call(
        _mm_kernel, grid=grid, in_specs=in_specs,
        out_specs=pl.BlockSpec((tm, tn), lambda i, j: (i, j)),
        out_shape=jax.ShapeDtypeStruct((m, n), F32),
        compiler_params=_cparams(("parallel", "arbitrary")),
        name="proj",
    )(a, b)


def _ffn_in_kernel(x_ref, wg_ref, wu_ref, o_ref):
    x = x_ref[...]
    g = jnp.dot(x, wg_ref[...], preferred_element_type=F32)
    u = jnp.dot(x, wu_ref[...], preferred_element_type=F32)
    o_ref[...] = ((g * _sigmoid(g)) * u).astype(o_ref.dtype)


def _ffn_in(x, wg, wu):
    m, k = x.shape
    f = wg.shape[1]
    tm = _pick(m, (1024, 512, 256, 128, 64, 32, 16))
    tf = _pick(f, (512, 256, 128))
    return pl.pallas_call(
        _ffn_in_kernel,
        grid=(m // tm, f // tf),
        in_specs=[pl.BlockSpec((tm, k), lambda i, j: (i, 0)),
                  pl.BlockSpec((k, tf), lambda i, j: (0, j)),
                  pl.BlockSpec((k, tf), lambda i, j: (0, j))],
        out_specs=pl.BlockSpec((tm, tf), lambda i, j: (i, j)),
        out_shape=jax.ShapeDtypeStruct((m, f), BF16),
        compiler_params=_cparams(("parallel", "arbitrary")),
        name="ffn_in",
    )(x, wg, wu)


def _mm_res_kernel(a_ref, b_ref, r_ref, o_ref, *, scale):
    @pl.when(pl.program_id(2) == 0)
    def _():
        o_ref[...] = r_ref[...]

    o_ref[...] += scale * jnp.dot(a_ref[...], b_ref[...], preferred_element_type=F32)


def _matmul_residual(a, b, res, scale):
    m, k = a.shape
    n = b.shape[1]
    tm = _pick(m, (1024, 512, 256, 128, 64, 32, 16))
    tn = _pick(n, (2048, 1024, 512, 256, 128))
    tk = _pick(k, (1024, 512, 256, 128))
    return pl.pallas_call(
        functools.partial(_mm_res_kernel, scale=scale),
        grid=(m // tm, n // tn, k // tk),
        in_specs=[pl.BlockSpec((tm, tk), lambda i, j, kk: (i, kk)),
                  pl.BlockSpec((tk, tn), lambda i, j, kk: (kk, j)),
                  pl.BlockSpec((tm, tn), lambda i, j, kk: (i, j))],
        out_specs=pl.BlockSpec((tm, tn), lambda i, j, kk: (i, j)),
        out_shape=jax.ShapeDtypeStruct((m, n), F32),
        compiler_params=_cparams(("parallel", "parallel", "arbitrary")),
        name="proj_residual",
    )(a, b, res)


def _merge_kernel(u_ref, oh_ref, oa_ref, wgh_ref, wga_ref, wbh_ref, wba_ref, o_ref):
    u = u_ref[...]
    oh = jnp.concatenate([oh_ref[h] for h in range(oh_ref.shape[0])], axis=1)
    gh = jnp.dot(u, wgh_ref[...], preferred_element_type=F32)
    ga = jnp.dot(u, wga_ref[...], preferred_element_type=F32)
    bh = jnp.dot(oh, wbh_ref[...], preferred_element_type=F32)
    ba = jnp.dot(oa_ref[...], wba_ref[...], preferred_element_type=F32)
    o_ref[...] = (_sigmoid(gh) * bh + _sigmoid(ga) * ba).astype(o_ref.dtype)


def _merge(u, oh, oa, wgh, wga, wbh, wba):
    m, d = u.shape
    nh = oh.shape[0]
    n = wgh.shape[1]
    tm = _pick(m, (512, 256, 128, 64, 32, 16))
    tn = _pick(n, (256, 128))
    return pl.pallas_call(
        _merge_kernel,
        grid=(m // tm, n // tn),
        in_specs=[pl.BlockSpec((tm, d), lambda i, j: (i, 0)),
                  pl.BlockSpec((nh, tm, LANES), lambda i, j: (0, i, 0)),
                  pl.BlockSpec((tm, oa.shape[1]), lambda i, j: (i, 0)),
                  pl.BlockSpec((d, tn), lambda i, j: (0, j)),
                  pl.BlockSpec((d, tn), lambda i, j: (0, j)),
                  pl.BlockSpec((wbh.shape[0], tn), lambda i, j: (0, j)),
                  pl.BlockSpec((wba.shape[0], tn), lambda i, j: (0, j))],
        out_specs=pl.BlockSpec((tm, tn), lambda i, j: (i, j)),
        out_shape=jax.ShapeDtypeStruct((m, n), BF16),
        compiler_params=_cparams(("parallel", "arbitrary")),
        name="branch_merge",
    )(u, oh, oa, wgh, wga, wbh, wba)


def _rope_tail(y2, tab):
    w = y2 * tab
    rot = w + pltpu.roll(w, QK_ROPE, axis=1)
    lane = lax.broadcasted_iota(jnp.int32, rot.shape, 1)
    return jnp.where(lane < QK_ROPE, rot, 0.0)


def _head_sumsq(y1, y2):
    lane = lax.broadcasted_iota(jnp.int32, y2.shape, 1)
    y2m = jnp.where(lane < QK_ROPE, y2, 0.0)
    return jnp.sum(y1 * y1, axis=-1, keepdims=True) + jnp.sum(y2m * y2m, axis=-1, keepdims=True)


def _latent_norm(x, w):
    ms = jnp.mean(x * x, axis=-1, keepdims=True)
    return ((x * lax.rsqrt(ms + RMS_EPS)) * w).astype(BF16)


def _qkv_kernel(lat_ref, qlw_ref, kvlw_ref, wq_ref, wkv_ref, gq_ref, gk_ref, tab_ref,
                q_ref, k_ref, v_ref, *, q_lora, kv_lora, q_scale):
    xq = _latent_norm(lat_ref[:, 0:q_lora], qlw_ref[...])
    xkv = _latent_norm(lat_ref[:, q_lora:q_lora + kv_lora], kvlw_ref[...])
    kr = lat_ref[:, q_lora + kv_lora:q_lora + kv_lora + LANES]
    tab = tab_ref[...]
    gq, gk = gq_ref[...], gk_ref[...]

    def body(h, carry):
        y = jnp.dot(xq, wq_ref[h], preferred_element_type=F32)
        y1, y2 = y[:, :QK_NOPE], y[:, QK_NOPE:]
        r = lax.rsqrt(_head_sumsq(y1, y2) * (1.0 / QK_HEAD) + RMS_EPS)
        q_ref[h, :, 0:QK_NOPE] = (((y1 * r) * gq[:, :QK_NOPE]) * q_scale).astype(BF16)
        q_ref[h, :, QK_NOPE:] = (_rope_tail((y2 * r) * gq[:, QK_NOPE:], tab) * q_scale).astype(BF16)

        z = jnp.dot(xkv, wkv_ref[h], preferred_element_type=F32)
        z1 = z[:, :QK_NOPE]
        rk = lax.rsqrt(_head_sumsq(z1, kr) * (1.0 / QK_HEAD) + RMS_EPS)
        k_ref[h, :, 0:QK_NOPE] = ((z1 * rk) * gk[:, :QK_NOPE]).astype(BF16)
        k_ref[h, :, QK_NOPE:] = _rope_tail((kr * rk) * gk[:, QK_NOPE:], tab).astype(BF16)
        v_ref[h] = z[:, QK_NOPE:].astype(BF16)
        return carry

    lax.fori_loop(0, q_ref.shape[0], body, 0)


def _qkv_proj(lat, q_lat_w, kv_lat_w, wq, wkv, gq, gk, tab, row_off, rows, q_lora, kv_lora):
    nh = wq.shape[0]
    tl = _pick(math.gcd(rows, row_off) if row_off else rows, (512, 256, 128, 64, 32, 16))
    off = row_off // tl
    full = lambda shape: pl.BlockSpec(shape, lambda i: (0,) * len(shape))
    return pl.pallas_call(
        functools.partial(_qkv_kernel, q_lora=q_lora, kv_lora=kv_lora,
                          q_scale=LOG2E / math.sqrt(QK_HEAD)),
        grid=(rows // tl,),
        in_specs=[pl.BlockSpec((tl, lat.shape[1]), lambda i: (off + i, 0)),
                  full((1, q_lora)), full((1, kv_lora)),
                  full((nh, q_lora, QK_PAD)), full((nh, kv_lora, QK_NOPE + V_HEAD)),
                  full((1, QK_PAD)), full((1, QK_PAD)),
                  pl.BlockSpec((tl, LANES), lambda i: (i, 0))],
        out_specs=[pl.BlockSpec((nh, tl, QK_PAD), lambda i: (0, i, 0)),
                   pl.BlockSpec((nh, tl, QK_PAD), lambda i: (0, i, 0)),
                   pl.BlockSpec((nh, tl, V_HEAD), lambda i: (0, i, 0))],
        out_shape=[jax.ShapeDtypeStruct((nh, rows, QK_PAD), BF16),
                   jax.ShapeDtypeStruct((nh, rows, QK_PAD), BF16),
                   jax.ShapeDtypeStruct((nh, rows, V_HEAD), BF16)],
        compiler_params=_cparams(("parallel",)),
        name="mla_qkv_proj",
    )(lat, q_lat_w, kv_lat_w, wq, wkv, gq, gk, tab)


def _attn_kernel(q_ref, k_ref, v_ref, o_ref, *, n_valid, n_sub):
    rows = q_ref.shape[1] // n_sub
    for r in range(n_sub):
        sl = slice(r * rows, (r + 1) * rows)
        o_ref[sl, :] = _attn_rows(q_ref[0, sl, :], k_ref, v_ref, n_valid).astype(o_ref.dtype)


def _attn_rows(q, k_ref, v_ref, n_valid):
    lp = k_ref.shape[1]
    s = lax.dot_general(q, k_ref[0], (((1,), (1,)), ((), ())), preferred_element_type=F32)
    t0 = (n_valid // LANES) * LANES
    if t0 == lp:
        p = jnp.exp2(s - jnp.max(s, axis=-1, keepdims=True))
        denom = jnp.sum(p, axis=-1, keepdims=True)
        o = jnp.dot(p.astype(BF16), v_ref[0], preferred_element_type=F32)
    else:
        s_main, s_tail = s[:, :t0], s[:, t0:]
        col = lax.broadcasted_iota(jnp.int32, s_tail.shape, 1)
        s_tail = jnp.where(col < n_valid - t0, s_tail, NEG_BIG)
        m = jnp.max(s_tail, axis=-1, keepdims=True)
        if t0 > 0:
            m = jnp.maximum(m, jnp.max(s_main, axis=-1, keepdims=True))
        p_tail = jnp.exp2(s_tail - m)
        denom = jnp.sum(p_tail, axis=-1, keepdims=True)
        o = jnp.dot(p_tail.astype(BF16), v_ref[0, t0:, :], preferred_element_type=F32)
        if t0 > 0:
            p_main = jnp.exp2(s_main - m)
            denom = denom + jnp.sum(p_main, axis=-1, keepdims=True)
            o = o + jnp.dot(p_main.astype(BF16), v_ref[0, :t0, :], preferred_element_type=F32)
    return o / denom


def _attention(q, k, v, batch, lp, n_valid):
    nh = q.shape[0]
    nq = next(n for n in range(1, lp + 1) if lp % n == 0 and (lp // n) % 16 == 0 and lp // n <= ATTN_MAX_Q_ROWS)
    tq = lp // nq
    n_sub = next((n for n in (3, 2, 4) if tq % (16 * n) == 0), 1)
    return pl.pallas_call(
        functools.partial(_attn_kernel, n_valid=n_valid, n_sub=n_sub),
        grid=(batch, nh, nq),
        in_specs=[pl.BlockSpec((1, tq, QK_PAD), lambda b, h, i: (h, b * nq + i, 0)),
                  pl.BlockSpec((1, lp, QK_PAD), lambda b, h, i: (h, b, 0)),
                  pl.BlockSpec((1, lp, V_HEAD), lambda b, h, i: (h, b, 0))],
        out_specs=pl.BlockSpec((tq, V_HEAD), lambda b, h, i: (b * nq + i, h)),
        out_shape=jax.ShapeDtypeStruct((batch * lp, nh * V_HEAD), BF16),
        compiler_params=_cparams(("parallel", "parallel", "arbitrary")),
        name="mla_attention",
    )(q, k, v)


def _lower_bound(logit_ref, h, layer):
    lg = logit_ref[0, h]
    e = jnp.exp(lg - jnp.max(lg, axis=0, keepdims=True))
    p = e / jnp.sum(e, axis=0, keepdims=True)
    lb = jnp.zeros((1, lg.shape[1]), F32)
    for j in range(1, layer + 1):
        lb = lb + p[j:j + 1]
    return lb


def _gates(zq, zf, zi, lb, row_ok):
    q = zq * _sigmoid(zq)
    e = jnp.exp(-jnp.abs(zf))
    r = 1.0 / (1.0 + e)
    er = e * r
    pos = zf >= 0.0
    k = jnp.where(row_ok, (1.0 - lb) * jnp.where(pos, er, r), 0.0)
    f = lb + (1.0 - lb) * jnp.where(pos, r, er)
    g = jnp.maximum(jnp.log(f), jnp.log1p(-lb) + jnp.minimum(zf, 0.0) - LN2)
    return q, k, zi, g


def _log_cumsum(g, tri):
    g1 = g.astype(BF16)
    r1 = g - g1.astype(F32)
    g2 = r1.astype(BF16)
    g3 = (r1 - g2.astype(F32)).astype(BF16)
    bb = jnp.dot(tri, jnp.concatenate([g1, g2, g3], axis=1), preferred_element_type=F32)
    return (bb[:, :LANES] + bb[:, LANES:2 * LANES]) + bb[:, 2 * LANES:]


def _carried(q, k, v, b, total, st):
    qh = (q * jnp.exp(b)).astype(BF16)
    kh = (k * jnp.exp(total - b)).astype(BF16)
    o = lax.dot_general(qh, st.astype(BF16), (((1,), (1,)), ((), ())), preferred_element_type=F32)
    st_new = st * jnp.exp(total) + lax.dot_general(
        v.astype(BF16), kh, (((0,), (0,)), ((), ())), preferred_element_type=F32)
    return o, st_new


def _hgrn_chunk_factorised(q, k, v, g, st, tri, in_scan, reverse):
    c_rows = q.shape[0]
    sub = HG_SUB_ROWS
    b = _log_cumsum(g, tri)
    total = b[0:1] if reverse else b[c_rows - 1:c_rows]
    o, st_new = _carried(q, k, v, b, total, st)

    zero_row = jnp.zeros((1, LANES), F32)
    growth = zero_row
    rows = []
    for blk in range(c_rows // sub):
        lo, hi = blk * sub, (blk + 1) * sub
        if reverse:
            ref = b[hi:hi + 1] if hi < c_rows else zero_row
            growth = jnp.maximum(growth, ref - b[lo:lo + 1])
        else:
            ref = b[lo - 1:lo] if lo > 0 else zero_row
            growth = jnp.maximum(growth, ref - b[hi - 1:hi])
        qi = (q[lo:hi] * jnp.exp(b[lo:hi] - ref)).astype(BF16)
        ks = (k * jnp.exp(jnp.minimum(ref - b, HG_MAX_LOG_GROWTH))).astype(BF16)
        rows.append(lax.dot_general(qi, ks, (((1,), (1,)), ((), ())), preferred_element_type=F32))
    a = jnp.where(in_scan, jnp.concatenate(rows, axis=0), 0.0)
    o = o + jnp.dot(a.astype(BF16), v.astype(BF16), preferred_element_type=F32)
    return o, st_new, growth


def _hgrn_chunk_exact(q, k, v, g, st, tri, ones_sq, reverse):
    c_rows = q.shape[0]
    sub = HG_EXACT_SUB_ROWS
    nb = c_rows // sub
    b = _log_cumsum(g, tri)
    total = b[0:1] if reverse else b[c_rows - 1:c_rows]
    o, _ = _carried(q, k, v, b, total, st)

    vb = v.astype(BF16)
    pieces = []
    for blk in range(nb):
        lo, hi = blk * sub, (blk + 1) * sub
        if reverse:
            if blk == nb - 1:
                pieces.append(jnp.zeros((sub, LANES), F32))
                continue
            ref = b[hi:hi + 1]
            other = slice(hi, c_rows)
        else:
            if blk == 0:
                pieces.append(jnp.zeros((sub, LANES), F32))
                continue
            ref = b[lo - 1:lo]
            other = slice(0, lo)
        qi = (q[lo:hi] * jnp.exp(b[lo:hi] - ref)).astype(BF16)
        ko = (k[other] * jnp.exp(ref - b[other])).astype(BF16)
        a = lax.dot_general(qi, ko, (((1,), (1,)), ((), ())), preferred_element_type=F32)
        pieces.append(jnp.dot(a.astype(BF16), vb[other], preferred_element_type=F32))
    o = o + jnp.concatenate(pieces, axis=0)

    pos = lax.broadcasted_iota(jnp.int32, (c_rows, LANES), 0) % sub
    d_list, v_list = [], []
    for d in range(sub):
        if d == 0:
            vs = v
            dmat = q * k
        else:
            shift = (c_rows - d) if reverse else d
            ks = pltpu.roll(k, shift, axis=0)
            bs = pltpu.roll(b, shift, axis=0)
            vs = pltpu.roll(v, shift, axis=0)
            ok = (pos + d < sub) if reverse else (pos >= d)
            dmat = (q * ks) * jnp.exp(jnp.where(ok, b - bs, NEG_BIG))
        d_list.append(dmat.astype(BF16))
        v_list.append(vs)
    rs = jnp.dot(jnp.concatenate(d_list, axis=0), ones_sq, preferred_element_type=F32)
    for d in range(sub):
        o = o + rs[d * c_rows:(d + 1) * c_rows] * v_list[d]
    return o


def _hgrn_kernel(info_ref, *refs, layer, reverse):
    if reverse:
        zq_ref, zf_ref, zi_ref, zg_ref, of_ref, lg_ref, nw_ref, o_ref, st_ref = refs
    else:
        zq_ref, zf_ref, zi_ref, lg_ref, o_ref, st_ref = refs
    step = pl.program_id(0)
    g = (pl.num_programs(0) - 1 - step) if reverse else step
    local, n_chunks, seq_rows = info_ref[3 * g], info_ref[3 * g + 1], info_ref[3 * g + 2]
    nh, c_rows = zq_ref.shape[0], zq_ref.shape[1]
    cur = step % 2

    @pl.when((local == n_chunks - 1) if reverse else (local == 0))
    def _():
        st_ref[cur] = jnp.zeros(st_ref.shape[1:], F32)

    rr = lax.broadcasted_iota(jnp.int32, (c_rows, c_rows), 0)
    cc = lax.broadcasted_iota(jnp.int32, (c_rows, c_rows), 1)
    in_scan = (cc >= rr) if reverse else (cc <= rr)
    tri = jnp.where(in_scan, 1.0, 0.0).astype(BF16)
    row_ok = (lax.broadcasted_iota(jnp.int32, (c_rows, LANES), 0) + local * c_rows) < seq_rows

    def finish(h, o):
        if reverse:
            o = o + of_ref[h]
            ms = jnp.mean(o * o, axis=-1, keepdims=True)
            zg = zg_ref[h]
            o_ref[h] = (((o * lax.rsqrt(ms + RMS_EPS)) * nw_ref[...]) * (zg * _sigmoid(zg))).astype(o_ref.dtype)
        else:
            o_ref[h] = o

    def fast_body(h, growth):
        q, k, v, gl = _gates(zq_ref[h], zf_ref[h], zi_ref[h], _lower_bound(lg_ref, h, layer), row_ok)
        o, st_new, gr = _hgrn_chunk_factorised(q, k, v, gl, st_ref[cur, h], tri, in_scan, reverse)
        st_ref[1 - cur, h] = st_new
        finish(h, o)
        return jnp.maximum(growth, gr)

    growth = lax.fori_loop(0, nh, fast_body, jnp.zeros((1, LANES), F32), unroll=4)

    @pl.when(jnp.max(growth) > HG_MAX_LOG_GROWTH)
    def _():
        ones_sq = jnp.ones((LANES, LANES), BF16)

        def exact_body(h, carry):
            q, k, v, gl = _gates(zq_ref[h], zf_ref[h], zi_ref[h], _lower_bound(lg_ref, h, layer), row_ok)
            finish(h, _hgrn_chunk_exact(q, k, v, gl, st_ref[cur, h], tri, ones_sq, reverse))
            return carry

        lax.fori_loop(0, nh, exact_body, 0)


def _hgrn(hg, info, lb_logits, out_norm, layer, nh):
    m = hg.shape[1]
    c = HG_CHUNK_ROWS
    n_chunks = m // c
    depth = lb_logits.shape[1]
    lg = lb_logits.astype(F32).reshape(2, depth, nh, LANES).transpose(0, 2, 1, 3)

    def blk(grp, rev):
        if rev:
            return pl.BlockSpec((nh, c, LANES), lambda g, info_ref: (grp, n_chunks - 1 - g, 0))
        return pl.BlockSpec((nh, c, LANES), lambda g, info_ref: (grp, g, 0))

    state = pltpu.VMEM((2, nh, LANES, LANES), F32)
    o_f = pl.pallas_call(
        functools.partial(_hgrn_kernel, layer=layer, reverse=False),
        grid_spec=pltpu.PrefetchScalarGridSpec(
            num_scalar_prefetch=1, grid=(n_chunks,),
            in_specs=[blk(0, False), blk(1, False), blk(3, False),
                      pl.BlockSpec((1, nh, depth, LANES), lambda g, info_ref: (0, 0, 0, 0))],
            out_specs=blk(0, False),
            scratch_shapes=[state]),
        out_shape=jax.ShapeDtypeStruct((nh, m, LANES), F32),
        compiler_params=_cparams(("arbitrary",)),
        name="hgrn_forward",
    )(info, hg, hg, hg, lg)

    return pl.pallas_call(
        functools.partial(_hgrn_kernel, layer=layer, reverse=True),
        grid_spec=pltpu.PrefetchScalarGridSpec(
            num_scalar_prefetch=1, grid=(n_chunks,),
            in_specs=[blk(0, True), blk(2, True), blk(3, True), blk(4, True), blk(0, True),
                      pl.BlockSpec((1, nh, depth, LANES), lambda g, info_ref: (1, 0, 0, 0)),
                      pl.BlockSpec((1, LANES), lambda g, info_ref: (0, 0))],
            out_specs=blk(0, True),
            scratch_shapes=[state]),
        out_shape=jax.ShapeDtypeStruct((nh, m, LANES), BF16),
        compiler_params=_cparams(("arbitrary",)),
        name="hgrn_backward",
    )(info, hg, hg, hg, hg, o_f, lg, out_norm.reshape(1, LANES).astype(F32))


def _swap_halves(x, axis):
    a, b = jnp.split(x, 2, axis=axis)
    return jnp.concatenate([b, a], axis=axis)


def _rope_table(rows):
    inv_freq = 1.0 / (ROPE_THETA ** (jnp.arange(0, QK_ROPE, 2, dtype=F32) / QK_ROPE))
    ang = jnp.arange(rows, dtype=F32)[:, None] * inv_freq[None, :]
    c, s = jnp.cos(ang), jnp.sin(ang)
    return jnp.concatenate([c, c, -s, s], axis=1)


def _head_gain(g):
    g = g.astype(F32)
    return jnp.concatenate([g[:QK_NOPE], g[QK_NOPE:], _swap_halves(g[QK_NOPE:], 0)]).reshape(1, QK_PAD)


def _ffn_weights(w_in, w_out):
    d_ff = w_out.shape[0]
    f_pad = -(-d_ff // 1024) * 1024 if d_ff >= 1024 else -(-d_ff // LANES) * LANES
    pad = f_pad - d_ff
    wg = jnp.pad(w_in[:, :d_ff].astype(BF16), ((0, 0), (0, pad)))
    wu = jnp.pad(w_in[:, d_ff:].astype(BF16), ((0, 0), (0, pad)))
    wo = jnp.pad(w_out.astype(BF16), ((0, pad), (0, 0)))
    return wg, wu, wo


def _ffn(h, norm_w, weights, scale=0.5):
    wg, wu, wo = weights
    return _matmul_residual(_ffn_in(_rmsnorm(h, norm_w), wg, wu), wo, h, scale)


def kernel(x_prompt, x_sample, meta_tokens, hgrn_lb_logits, norm_ffn1, w_ffn1_in, w_ffn1_out, norm_mix, w_in,
           q_lat_norm, w_uq, kv_lat_norm, w_ukv, q_head_norm, k_head_norm, hg_out_norm,
           w_branch_hgrn, w_branch_mla, w_out, norm_ffn2, w_ffn2_in, w_ffn2_out):
    depth = norm_ffn1.shape[0]
    d_model = x_prompt.shape[-1]
    hg_dim = hgrn_lb_logits.shape[-1]
    hg_heads = hg_dim // HG_HEAD_DIM
    q_lora = q_lat_norm.shape[-1]
    kv_lora = kv_lat_norm.shape[-1]
    mla_heads = w_uq.shape[-1] // QK_HEAD

    trunks = []
    row_off = 0
    blocks, info = [], []
    for x in (x_prompt, x_sample):
        bsz, s, _ = x.shape
        lp = s + SEQ_PAD
        meta = jnp.broadcast_to(meta_tokens.astype(x.dtype)[None], (bsz, N_META, d_model))
        pad = jnp.zeros((bsz, lp - s - N_META, d_model), x.dtype)
        blocks.append(jnp.concatenate([meta, x, pad], axis=1).reshape(bsz * lp, d_model))
        nc = lp // HG_CHUNK_ROWS
        for _ in range(bsz):
            for c in range(nc):
                info += [c, nc, s + N_META]
        trunks.append((bsz, s, lp, row_off))
        row_off += bsz * lp
    h = jnp.concatenate(blocks, axis=0).astype(F32)
    info = jnp.asarray(info, jnp.int32)
    tabs = [jnp.tile(_rope_table(lp), (bsz, 1)) for (bsz, s, lp, _) in trunks]

    sizes = (hg_dim,) * 5 + (q_lora, kv_lora, QK_ROPE, d_model, d_model)
    offs = [0]
    for sz in sizes:
        offs.append(offs[-1] + sz)

    for l in range(depth):
        ffn1 = _ffn_weights(w_ffn1_in[l], w_ffn1_out[l])
        ffn2 = _ffn_weights(w_ffn2_in[l], w_ffn2_out[l])
        wl = w_in[l]
        w_hg = wl[:, :offs[5]].astype(BF16)
        w_kr = wl[:, offs[7]:offs[8]]
        w_lat = jnp.concatenate([wl[:, offs[5]:offs[7]], w_kr, _swap_halves(w_kr, 1)], axis=1).astype(BF16)
        w_gh = wl[:, offs[8]:offs[9]].astype(BF16)
        w_ga = wl[:, offs[9]:offs[10]].astype(BF16)
        wq = w_uq[l].reshape(q_lora, mla_heads, QK_HEAD)
        wq = jnp.concatenate([wq, _swap_halves(wq[:, :, QK_NOPE:], 2)], axis=2)
        wq = wq.transpose(1, 0, 2).astype(BF16)
        wkv = w_ukv[l].reshape(kv_lora, mla_heads, QK_NOPE + V_HEAD).transpose(1, 0, 2).astype(BF16)

        h = _ffn(h, norm_ffn1[l], ffn1)

        u = _rmsnorm(h, norm_mix[l])
        hg = _matmul(u, w_hg, head_major=True)
        lat = _matmul(u, w_lat)
        o_h = _hgrn(hg, info, hgrn_lb_logits, hg_out_norm[l], l, hg_heads)

        o_a = []
        for (bsz, s, lp, off), tab in zip(trunks, tabs):
            q, k, v = _qkv_proj(lat, q_lat_norm[l].reshape(1, q_lora).astype(F32),
                                kv_lat_norm[l].reshape(1, kv_lora).astype(F32), wq, wkv,
                                _head_gain(q_head_norm[l]), _head_gain(k_head_norm[l]),
                                tab, off, bsz * lp, q_lora, kv_lora)
            o_a.append(_attention(q, k, v, bsz, lp, s + N_META))
        o_a = jnp.concatenate(o_a, axis=0)

        merged = _merge(u, o_h, o_a, w_gh, w_ga, w_branch_hgrn[l].astype(BF16), w_branch_mla[l].astype(BF16))
        h = _matmul_residual(merged, w_out[l].astype(BF16), h, 1.0)

        h = _ffn(h, norm_ffn2[l], ffn2)

    outs = []
    for (bsz, s, lp, off) in trunks:
        outs.append(h[off:off + bsz * lp].reshape(bsz, lp, d_model)[:, N_META:N_META + s])
    return tuple(outs)
```

```python
import functools
import math

import jax
import jax.numpy as jnp
from jax import lax
from jax.experimental import pallas as pl
from jax.experimental.pallas import tpu as pltpu

N_META = 16
HG_HEAD_DIM = 128
QK_NOPE = 128
QK_ROPE = 64
QK_HEAD = QK_NOPE + QK_ROPE
V_HEAD = 128
ROPE_THETA = 10000.0
RMS_EPS = 1e-6

LANES = 128
SEQ_PAD = 128
HG_CHUNK_ROWS = 128
HG_SUB_ROWS = 32
HG_EXACT_SUB_ROWS = 16
HG_MAX_LOG2_GROWTH = 115.0
ATTN_MAX_Q_ROWS = 640
ATTN_KEY_BLOCKS = 3
QK_PAD = 256
VMEM_LIMIT = 56 * 1024 * 1024

F32 = jnp.float32
BF16 = jnp.bfloat16
NEG_BIG = -1e30
LN2 = math.log(2.0)
LOG2E = 1.0 / LN2


def _cparams(sem):
    return pltpu.CompilerParams(dimension_semantics=sem, vmem_limit_bytes=VMEM_LIMIT)


def _pick(n, prefs):
    for p in prefs:
        if n % p == 0:
            return p
    return n


def _sigmoid(x):
    return 1.0 / (1.0 + jnp.exp(-x))


def _rmsnorm_kernel(x_ref, w_ref, o_ref):
    x = x_ref[...]
    ms = jnp.mean(x * x, axis=-1, keepdims=True)
    o_ref[...] = ((x * lax.rsqrt(ms + RMS_EPS)) * w_ref[...]).astype(o_ref.dtype)


def _rmsnorm(x, w):
    m, d = x.shape
    tm = _pick(m, (512, 256, 128, 64, 32, 16, 8))
    return pl.pallas_call(
        _rmsnorm_kernel,
        grid=(m // tm,),
        in_specs=[pl.BlockSpec((tm, d), lambda i: (i, 0)),
                  pl.BlockSpec((1, d), lambda i: (0, 0))],
        out_specs=pl.BlockSpec((tm, d), lambda i: (i, 0)),
        out_shape=jax.ShapeDtypeStruct((m, d), BF16),
        compiler_params=_cparams(("parallel",)),
        name="rmsnorm",
    )(x, w.reshape(1, d).astype(F32))


def _mm_kernel(a_ref, b_ref, o_ref):
    o_ref[...] = jnp.dot(a_ref[...], b_ref[...], preferred_element_type=F32)


def _mm_headmajor_kernel(a_ref, b_ref, o_ref):
    acc = jnp.dot(a_ref[...], b_ref[...], preferred_element_type=F32)
    for j in range(o_ref.shape[0]):
        o_ref[j] = acc[:, j * LANES:(j + 1) * LANES]


def _matmul(a, b, *, head_major=False):
    m, k = a.shape
    n = b.shape[1]
    if n % 256 == 0:
        tm = _pick(m, (1024, 512, 256, 128, 64, 32, 16))
        tn = _pick(n, (512, 256))
    else:
        tm = _pick(m, (512, 256, 128, 64, 32, 16))
        tn = n
    grid = (m // tm, n // tn)
    in_specs = [pl.BlockSpec((tm, k), lambda i, j: (i, 0)),
                pl.BlockSpec((k, tn), lambda i, j: (0, j))]
    if head_major:
        return pl.pallas_call(
            _mm_headmajor_kernel, grid=grid, in_specs=in_specs,
            out_specs=pl.BlockSpec((tn // LANES, tm, LANES), lambda i, j: (j, i, 0)),
            out_shape=jax.ShapeDtypeStruct((n // LANES, m, LANES), F32),
            compiler_params=_cparams(("parallel", "arbitrary")),
            name="proj_headmajor",
        )(a, b)
    return pl.pallas_call(
        _mm_kernel, grid=grid, in_specs=in_specs,
        out_specs=pl.BlockSpec((tm, tn), lambda i, j: (i, j)),
        out_shape=jax.ShapeDtypeStruct((m, n), F32),
        compiler_params=_cparams(("parallel", "arbitrary")),
        name="proj",
    )(a, b)


def _ffn_in_kernel(x_ref, wg_ref, wu_ref, o_ref):
    x = x_ref[...]
    g = jnp.dot(x, wg_ref[...], preferred_element_type=F32)
    u = jnp.dot(x, wu_ref[...], preferred_element_type=F32)
    o_ref[...] = ((g * _sigmoid(g)) * u).astype(o_ref.dtype)


def _ffn_in(x, wgu):
    m, k = x.shape
    f = wgu.shape[1] // 2
    tm = _pick(m, (1024, 512, 256, 128, 64, 32, 16))
    tf = _pick(f, (512, 256, 128))
    nf = f // tf
    return pl.pallas_call(
        _ffn_in_kernel,
        grid=(m // tm, nf),
        in_specs=[pl.BlockSpec((tm, k), lambda i, j: (i, 0)),
                  pl.BlockSpec((k, tf), lambda i, j: (0, j)),
                  pl.BlockSpec((k, tf), lambda i, j: (0, nf + j))],
        out_specs=pl.BlockSpec((tm, tf), lambda i, j: (i, j)),
        out_shape=jax.ShapeDtypeStruct((m, f), BF16),
        compiler_params=_cparams(("parallel", "arbitrary")),
        name="ffn_in",
    )(x, wgu, wgu)


def _mm_res_kernel(a_ref, b_ref, r_ref, o_ref, *, scale):
    @pl.when(pl.program_id(2) == 0)
    def _():
        o_ref[...] = r_ref[...]

    o_ref[...] += scale * jnp.dot(a_ref[...], b_ref[...], preferred_element_type=F32)


def _matmul_residual(a, b, res, scale):
    m, k = a.shape
    n = b.shape[1]
    tm = _pick(m, (1024, 512, 256, 128, 64, 32, 16))
    tn = _pick(n, (2048, 1024, 512, 256, 128))
    tk = _pick(k, (1024, 512, 256, 128))
    return pl.pallas_call(
        functools.partial(_mm_res_kernel, scale=scale),
        grid=(m // tm, n // tn, k // tk),
        in_specs=[pl.BlockSpec((tm, tk), lambda i, j, kk: (i, kk)),
                  pl.BlockSpec((tk, tn), lambda i, j, kk: (kk, j)),
                  pl.BlockSpec((tm, tn), lambda i, j, kk: (i, j))],
        out_specs=pl.BlockSpec((tm, tn), lambda i, j, kk: (i, j)),
        out_shape=jax.ShapeDtypeStruct((m, n), F32),
        compiler_params=_cparams(("parallel", "parallel", "arbitrary")),
        name="proj_residual",
    )(a, b, res)


def _merge_kernel(u_ref, oh_ref, oa_ref, wgh_ref, wga_ref, wbh_ref, wba_ref, o_ref):
    u = u_ref[...]
    oh = jnp.concatenate([oh_ref[h] for h in range(oh_ref.shape[0])], axis=1)
    gh = jnp.dot(u, wgh_ref[...], preferred_element_type=F32)
    ga = jnp.dot(u, wga_ref[...], preferred_element_type=F32)
    bh = jnp.dot(oh, wbh_ref[...], preferred_element_type=F32)
    ba = jnp.dot(oa_ref[...], wba_ref[...], preferred_element_type=F32)
    o_ref[...] = (_sigmoid(gh) * bh + _sigmoid(ga) * ba).astype(o_ref.dtype)


def _merge(u, oh, oa, wgh, wga, wbh, wba):
    m, d = u.shape
    nh = oh.shape[0]
    n = wgh.shape[1]
    tm = _pick(m, (1024, 512, 256, 128, 64, 32, 16))
    tn = _pick(n, (256, 128))
    return pl.pallas_call(
        _merge_kernel,
        grid=(m // tm, n // tn),
        in_specs=[pl.BlockSpec((tm, d), lambda i, j: (i, 0)),
                  pl.BlockSpec((nh, tm, LANES), lambda i, j: (0, i, 0)),
                  pl.BlockSpec((tm, oa.shape[1]), lambda i, j: (i, 0)),
                  pl.BlockSpec((d, tn), lambda i, j: (0, j)),
                  pl.BlockSpec((d, tn), lambda i, j: (0, j)),
                  pl.BlockSpec((wbh.shape[0], tn), lambda i, j: (0, j)),
                  pl.BlockSpec((wba.shape[0], tn), lambda i, j: (0, j))],
        out_specs=pl.BlockSpec((tm, tn), lambda i, j: (i, j)),
        out_shape=jax.ShapeDtypeStruct((m, n), BF16),
        compiler_params=_cparams(("parallel", "arbitrary")),
        name="branch_merge",
    )(u, oh, oa, wgh, wga, wbh, wba)


def _rope_tail(y2, tab):
    w = y2 * tab
    rot = w + pltpu.roll(w, QK_ROPE, axis=1)
    lane = lax.broadcasted_iota(jnp.int32, rot.shape, 1)
    return jnp.where(lane < QK_ROPE, rot, 0.0)


def _rope_sumsq(y2):
    lane = lax.broadcasted_iota(jnp.int32, y2.shape, 1)
    y2m = jnp.where(lane < QK_ROPE, y2, 0.0)
    return jnp.sum(y2m * y2m, axis=-1, keepdims=True)


def _head_rms_scale(y, ones_mask, extra_ss):
    ss = jnp.dot((y * y).astype(BF16), ones_mask, preferred_element_type=F32)
    return lax.rsqrt((ss + extra_ss) * (1.0 / QK_HEAD) + RMS_EPS)


def _latent_norm(x, w):
    ms = jnp.mean(x * x, axis=-1, keepdims=True)
    return ((x * lax.rsqrt(ms + RMS_EPS)) * w).astype(BF16)


def _qkv_kernel(lat_ref, qlw_ref, kvlw_ref, wq_ref, wkv_ref, gq_ref, gk_ref, tab_ref,
                q_ref, k_ref, v_ref, *, q_lora, kv_lora, q_scale):
    xq = _latent_norm(lat_ref[:, 0:q_lora], qlw_ref[...])
    xkv = _latent_norm(lat_ref[:, q_lora:q_lora + kv_lora], kvlw_ref[...])
    kr = lat_ref[:, q_lora + kv_lora:q_lora + kv_lora + LANES]
    gq1 = gq_ref[:, :QK_NOPE] * q_scale
    gk1 = gk_ref[:, :QK_NOPE]
    tab_q = tab_ref[...] * (gq_ref[:, QK_NOPE:] * q_scale)
    tab_k = tab_ref[...] * gk_ref[:, QK_NOPE:]
    kr_ss = _rope_sumsq(kr)
    row = lax.broadcasted_iota(jnp.int32, (QK_PAD, LANES), 0)
    ones_q = jnp.where(row < QK_HEAD, 1.0, 0.0).astype(BF16)
    ones_k = jnp.ones((QK_NOPE, LANES), BF16)

    def body(h, carry):
        y = jnp.dot(xq, wq_ref[h], preferred_element_type=F32)
        y1, y2 = y[:, :QK_NOPE], y[:, QK_NOPE:]
        r = _head_rms_scale(y, ones_q, 0.0)
        q_ref[h, :, 0:QK_NOPE] = ((y1 * r) * gq1).astype(BF16)
        q_ref[h, :, QK_NOPE:] = _rope_tail(y2 * r, tab_q).astype(BF16)

        z = jnp.dot(xkv, wkv_ref[h], preferred_element_type=F32)
        z1 = z[:, :QK_NOPE]
        rk = _head_rms_scale(z1, ones_k, kr_ss)
        k_ref[h, :, 0:QK_NOPE] = ((z1 * rk) * gk1).astype(BF16)
        k_ref[h, :, QK_NOPE:] = _rope_tail(kr * rk, tab_k).astype(BF16)
        v_ref[h] = z[:, QK_NOPE:].astype(BF16)
        return carry

    lax.fori_loop(0, q_ref.shape[0], body, 0, unroll=2)


def _qkv_proj(lat, q_lat_w, kv_lat_w, wq, wkv, gq, gk, tab, row_off, rows, q_lora, kv_lora):
    nh = wq.shape[0]
    tl = _pick(math.gcd(rows, row_off) if row_off else rows, (512, 256, 128, 64, 32, 16))
    off = row_off // tl
    full = lambda shape: pl.BlockSpec(shape, lambda i: (0,) * len(shape))
    return pl.pallas_call(
        functools.partial(_qkv_kernel, q_lora=q_lora, kv_lora=kv_lora,
                          q_scale=LOG2E / math.sqrt(QK_HEAD)),
        grid=(rows // tl,),
        in_specs=[pl.BlockSpec((tl, lat.shape[1]), lambda i: (off + i, 0)),
                  full((1, q_lora)), full((1, kv_lora)),
                  full((nh, q_lora, QK_PAD)), full((nh, kv_lora, QK_NOPE + V_HEAD)),
                  full((1, QK_PAD)), full((1, QK_PAD)),
                  pl.BlockSpec((tl, LANES), lambda i: (i, 0))],
        out_specs=[pl.BlockSpec((nh, tl, QK_PAD), lambda i: (0, i, 0)),
                   pl.BlockSpec((nh, tl, QK_PAD), lambda i: (0, i, 0)),
                   pl.BlockSpec((nh, tl, V_HEAD), lambda i: (0, i, 0))],
        out_shape=[jax.ShapeDtypeStruct((nh, rows, QK_PAD), BF16),
                   jax.ShapeDtypeStruct((nh, rows, QK_PAD), BF16),
                   jax.ShapeDtypeStruct((nh, rows, V_HEAD), BF16)],
        compiler_params=_cparams(("parallel",)),
        name="mla_qkv_proj",
    )(lat, q_lat_w, kv_lat_w, wq, wkv, gq, gk, tab)


def _attn_kernel(q_ref, k_ref, v_ref, o_ref, *, n_valid, n_sub):
    rows = q_ref.shape[1] // n_sub
    for r in range(n_sub):
        sl = slice(r * rows, (r + 1) * rows)
        o_ref[sl, :] = _attn_rows(q_ref[0, sl, :], k_ref, v_ref, n_valid).astype(o_ref.dtype)


def _attn_rows(q, k_ref, v_ref, n_valid):
    lp = k_ref.shape[1]
    kb = -(-n_valid // (ATTN_KEY_BLOCKS * QK_PAD)) * QK_PAD
    edges = list(range(0, min(lp, n_valid), kb)) + [min(lp, -(-n_valid // LANES) * LANES)]
    m = denom = acc = None
    for c0, c1 in zip(edges[:-1], edges[1:]):
        s = lax.dot_general(q, k_ref[0, c0:c1, :], (((1,), (1,)), ((), ())), preferred_element_type=F32)
        if c1 > n_valid:
            col = lax.broadcasted_iota(jnp.int32, s.shape, 1) + c0
            s = jnp.where(col < n_valid, s, NEG_BIG)
        mb = jnp.max(s, axis=-1, keepdims=True)
        if m is None:
            m = mb
            p = jnp.exp2(s - m)
            denom = jnp.sum(p, axis=-1, keepdims=True)
            acc = jnp.dot(p.astype(BF16), v_ref[0, c0:c1, :], preferred_element_type=F32)
        else:
            m_new = jnp.maximum(m, mb)
            alpha = jnp.exp2(m - m_new)
            p = jnp.exp2(s - m_new)
            denom = alpha * denom + jnp.sum(p, axis=-1, keepdims=True)
            acc = alpha * acc + jnp.dot(p.astype(BF16), v_ref[0, c0:c1, :], preferred_element_type=F32)
            m = m_new
    return acc / denom


def _attention(q, k, v, batch, lp, n_valid):
    nh = q.shape[0]
    nq = next(n for n in range(1, lp + 1) if lp % n == 0 and (lp // n) % 16 == 0 and lp // n <= ATTN_MAX_Q_ROWS)
    tq = lp // nq
    n_sub = 1
    return pl.pallas_call(
        functools.partial(_attn_kernel, n_valid=n_valid, n_sub=n_sub),
        grid=(batch, nh, nq),
        in_specs=[pl.BlockSpec((1, tq, QK_PAD), lambda b, h, i: (h, b * nq + i, 0)),
                  pl.BlockSpec((1, lp, QK_PAD), lambda b, h, i: (h, b, 0)),
                  pl.BlockSpec((1, lp, V_HEAD), lambda b, h, i: (h, b, 0))],
        out_specs=pl.BlockSpec((tq, V_HEAD), lambda b, h, i: (b * nq + i, h)),
        out_shape=jax.ShapeDtypeStruct((batch * lp, nh * V_HEAD), BF16),
        compiler_params=_cparams(("parallel", "parallel", "arbitrary")),
        name="mla_attention",
    )(q, k, v)


def _lower_bound(logit_ref, h, layer):
    lg = logit_ref[0, h]
    e = jnp.exp(lg - jnp.max(lg, axis=0, keepdims=True))
    p = e / jnp.sum(e, axis=0, keepdims=True)
    lb = jnp.zeros((1, lg.shape[1]), F32)
    for j in range(1, layer + 1):
        lb = lb + p[j:j + 1]
    return lb


def _gates(zq, zf, zi, lb, row_ok):
    q = zq * _sigmoid(zq)
    e = jnp.exp(-jnp.abs(zf))
    r = 1.0 / (1.0 + e)
    er = e * r
    pos = zf >= 0.0
    k = jnp.where(row_ok, (1.0 - lb) * jnp.where(pos, er, r), 0.0)
    f = lb + (1.0 - lb) * jnp.where(pos, r, er)
    g = jnp.maximum(jnp.log(f) * LOG2E, (jnp.log1p(-lb) * LOG2E - 1.0) + jnp.minimum(zf, 0.0) * LOG2E)
    return q, k, zi, g


def _log_cumsum(g, tri):
    g1 = g.astype(BF16)
    r1 = g - g1.astype(F32)
    g2 = r1.astype(BF16)
    g3 = (r1 - g2.astype(F32)).astype(BF16)
    bb = jnp.dot(tri, jnp.concatenate([g1, g2, g3], axis=1), preferred_element_type=F32)
    return (bb[:, :LANES] + bb[:, LANES:2 * LANES]) + bb[:, 2 * LANES:]


def _carried(q, k, v, b, total, st):
    qh = (q * jnp.exp2(b)).astype(BF16)
    kh = (k * jnp.exp2(total - b)).astype(BF16)
    o = lax.dot_general(qh, st.astype(BF16), (((1,), (1,)), ((), ())), preferred_element_type=F32)
    st_new = st * jnp.exp2(total) + lax.dot_general(
        v.astype(BF16), kh, (((0,), (0,)), ((), ())), preferred_element_type=F32)
    return o, st_new


def _hgrn_chunk_factorised(q, k, v, g, st, tri, in_scan, reverse):
    c_rows = q.shape[0]
    sub = HG_SUB_ROWS
    b = _log_cumsum(g, tri)
    total = b[0:1] if reverse else b[c_rows - 1:c_rows]
    o, st_new = _carried(q, k, v, b, total, st)

    zero_row = jnp.zeros((1, LANES), F32)
    growth = zero_row
    rows = []
    for blk in range(c_rows // sub):
        lo, hi = blk * sub, (blk + 1) * sub
        if reverse:
            ref = b[hi:hi + 1] if hi < c_rows else zero_row
            growth = jnp.maximum(growth, ref - b[lo:lo + 1])
        else:
            ref = b[lo - 1:lo] if lo > 0 else zero_row
            growth = jnp.maximum(growth, ref - b[hi - 1:hi])
        qi = (q[lo:hi] * jnp.exp2(b[lo:hi] - ref)).astype(BF16)
        ks = (k * jnp.exp2(jnp.minimum(ref - b, HG_MAX_LOG2_GROWTH))).astype(BF16)
        rows.append(lax.dot_general(qi, ks, (((1,), (1,)), ((), ())), preferred_element_type=F32))
    a = jnp.where(in_scan, jnp.concatenate(rows, axis=0), 0.0)
    o = o + jnp.dot(a.astype(BF16), v.astype(BF16), preferred_element_type=F32)
    return o, st_new, growth


def _hgrn_chunk_exact(q, k, v, g, st, tri, ones_sq, reverse):
    c_rows = q.shape[0]
    sub = HG_EXACT_SUB_ROWS
    nb = c_rows // sub
    b = _log_cumsum(g, tri)
    total = b[0:1] if reverse else b[c_rows - 1:c_rows]
    o, _ = _carried(q, k, v, b, total, st)

    vb = v.astype(BF16)
    pieces = []
    for blk in range(nb):
        lo, hi = blk * sub, (blk + 1) * sub
        if reverse:
            if blk == nb - 1:
                pieces.append(jnp.zeros((sub, LANES), F32))
                continue
            ref = b[hi:hi + 1]
            other = slice(hi, c_rows)
        else:
            if blk == 0:
                pieces.append(jnp.zeros((sub, LANES), F32))
                continue
            ref = b[lo - 1:lo]
            other = slice(0, lo)
        qi = (q[lo:hi] * jnp.exp2(b[lo:hi] - ref)).astype(BF16)
        ko = (k[other] * jnp.exp2(ref - b[other])).astype(BF16)
        a = lax.dot_general(qi, ko, (((1,), (1,)), ((), ())), preferred_element_type=F32)
        pieces.append(jnp.dot(a.astype(BF16), vb[other], preferred_element_type=F32))
    o = o + jnp.concatenate(pieces, axis=0)

    pos = lax.broadcasted_iota(jnp.int32, (c_rows, LANES), 0) % sub
    d_list, v_list = [], []
    for d in range(sub):
        if d == 0:
            vs = v
            dmat = q * k
        else:
            shift = (c_rows - d) if reverse else d
            ks = pltpu.roll(k, shift, axis=0)
            bs = pltpu.roll(b, shift, axis=0)
            vs = pltpu.roll(v, shift, axis=0)
            ok = (pos + d < sub) if reverse else (pos >= d)
            dmat = (q * ks) * jnp.exp2(jnp.where(ok, b - bs, NEG_BIG))
        d_list.append(dmat.astype(BF16))
        v_list.append(vs)
    rs = jnp.dot(jnp.concatenate(d_list, axis=0), ones_sq, preferred_element_type=F32)
    for d in range(sub):
        o = o + rs[d * c_rows:(d + 1) * c_rows] * v_list[d]
    return o


def _hgrn_kernel(info_ref, *refs, layer, reverse):
    if reverse:
        zq_ref, zf_ref, zi_ref, zg_ref, of_ref, lg_ref, nw_ref, o_ref, st_ref = refs
    else:
        zq_ref, zf_ref, zi_ref, lg_ref, o_ref, st_ref = refs
    step = pl.program_id(0)
    g = (pl.num_programs(0) - 1 - step) if reverse else step
    local, n_chunks, seq_rows = info_ref[3 * g], info_ref[3 * g + 1], info_ref[3 * g + 2]
    nh, c_rows = zq_ref.shape[0], zq_ref.shape[1]
    cur = step % 2

    @pl.when((local == n_chunks - 1) if reverse else (local == 0))
    def _():
        st_ref[cur] = jnp.zeros(st_ref.shape[1:], F32)

    rr = lax.broadcasted_iota(jnp.int32, (c_rows, c_rows), 0)
    cc = lax.broadcasted_iota(jnp.int32, (c_rows, c_rows), 1)
    in_scan = (cc >= rr) if reverse else (cc <= rr)
    tri = jnp.where(in_scan, 1.0, 0.0).astype(BF16)
    row_ok = (lax.broadcasted_iota(jnp.int32, (c_rows, LANES), 0) + local * c_rows) < seq_rows

    def finish(h, o):
        if reverse:
            o = o + of_ref[h]
            ms = jnp.mean(o * o, axis=-1, keepdims=True)
            zg = zg_ref[h]
            o_ref[h] = (((o * lax.rsqrt(ms + RMS_EPS)) * nw_ref[...]) * (zg * _sigmoid(zg))).astype(o_ref.dtype)
        else:
            o_ref[h] = o

    def fast_body(h, growth):
        q, k, v, gl = _gates(zq_ref[h], zf_ref[h], zi_ref[h], _lower_bound(lg_ref, h, layer), row_ok)
        o, st_new, gr = _hgrn_chunk_factorised(q, k, v, gl, st_ref[cur, h], tri, in_scan, reverse)
        st_ref[1 - cur, h] = st_new
        finish(h, o)
        return jnp.maximum(growth, gr)

    growth = lax.fori_loop(0, nh, fast_body, jnp.zeros((1, LANES), F32), unroll=16)

    @pl.when(jnp.max(growth) > HG_MAX_LOG2_GROWTH)
    def _():
        ones_sq = jnp.ones((LANES, LANES), BF16)

        def exact_body(h, carry):
            q, k, v, gl = _gates(zq_ref[h], zf_ref[h], zi_ref[h], _lower_bound(lg_ref, h, layer), row_ok)
            finish(h, _hgrn_chunk_exact(q, k, v, gl, st_ref[cur, h], tri, ones_sq, reverse))
            return carry

        lax.fori_loop(0, nh, exact_body, 0)


def _hgrn(hg, info, lb_logits, out_norm, layer, nh):
    m = hg.shape[1]
    c = HG_CHUNK_ROWS
    n_chunks = m // c
    depth = lb_logits.shape[1]
    lg = lb_logits.astype(F32).reshape(2, depth, nh, LANES).transpose(0, 2, 1, 3)

    def blk(grp, rev):
        if rev:
            return pl.BlockSpec((nh, c, LANES), lambda g, info_ref: (grp, n_chunks - 1 - g, 0))
        return pl.BlockSpec((nh, c, LANES), lambda g, info_ref: (grp, g, 0))

    state = pltpu.VMEM((2, nh, LANES, LANES), F32)
    o_f = pl.pallas_call(
        functools.partial(_hgrn_kernel, layer=layer, reverse=False),
        grid_spec=pltpu.PrefetchScalarGridSpec(
            num_scalar_prefetch=1, grid=(n_chunks,),
            in_specs=[blk(0, False), blk(1, False), blk(3, False),
                      pl.BlockSpec((1, nh, depth, LANES), lambda g, info_ref: (0, 0, 0, 0))],
            out_specs=blk(0, False),
            scratch_shapes=[state]),
        out_shape=jax.ShapeDtypeStruct((nh, m, LANES), F32),
        compiler_params=_cparams(("arbitrary",)),
        name="hgrn_forward",
    )(info, hg, hg, hg, lg)

    return pl.pallas_call(
        functools.partial(_hgrn_kernel, layer=layer, reverse=True),
        grid_spec=pltpu.PrefetchScalarGridSpec(
            num_scalar_prefetch=1, grid=(n_chunks,),
            in_specs=[blk(0, True), blk(2, True), blk(3, True), blk(4, True), blk(0, True),
                      pl.BlockSpec((1, nh, depth, LANES), lambda g, info_ref: (1, 0, 0, 0)),
                      pl.BlockSpec((1, LANES), lambda g, info_ref: (0, 0))],
            out_specs=blk(0, True),
            scratch_shapes=[state]),
        out_shape=jax.ShapeDtypeStruct((nh, m, LANES), BF16),
        compiler_params=_cparams(("arbitrary",)),
        name="hgrn_backward",
    )(info, hg, hg, hg, hg, o_f, lg, out_norm.reshape(1, LANES).astype(F32))


def _swap_halves(x, axis):
    a, b = jnp.split(x, 2, axis=axis)
    return jnp.concatenate([b, a], axis=axis)


def _rope_table(rows):
    inv_freq = 1.0 / (ROPE_THETA ** (jnp.arange(0, QK_ROPE, 2, dtype=F32) / QK_ROPE))
    ang = jnp.arange(rows, dtype=F32)[:, None] * inv_freq[None, :]
    c, s = jnp.cos(ang), jnp.sin(ang)
    return jnp.concatenate([c, c, -s, s], axis=1)


def _head_gain(g):
    g = g.astype(F32)
    return jnp.concatenate([g[:QK_NOPE], g[QK_NOPE:], _swap_halves(g[QK_NOPE:], 0)]).reshape(1, QK_PAD)


def _ffn_weights(w_in, w_out):
    d_ff = w_out.shape[0]
    f_pad = -(-d_ff // 1024) * 1024 if d_ff >= 1024 else -(-d_ff // LANES) * LANES
    pad = f_pad - d_ff
    d = w_in.shape[0]
    wgu = jnp.pad(w_in.reshape(d, 2, d_ff).astype(BF16), ((0, 0), (0, 0), (0, pad))).reshape(d, 2 * f_pad)
    wo = jnp.pad(w_out.astype(BF16), ((0, pad), (0, 0)))
    return wgu, wo


def _ffn(h, norm_w, weights, scale=0.5):
    wgu, wo = weights
    return _matmul_residual(_ffn_in(_rmsnorm(h, norm_w), wgu), wo, h, scale)


def kernel(x_prompt, x_sample, meta_tokens, hgrn_lb_logits, norm_ffn1, w_ffn1_in, w_ffn1_out, norm_mix, w_in,
           q_lat_norm, w_uq, kv_lat_norm, w_ukv, q_head_norm, k_head_norm, hg_out_norm,
           w_branch_hgrn, w_branch_mla, w_out, norm_ffn2, w_ffn2_in, w_ffn2_out):
    depth = norm_ffn1.shape[0]
    d_model = x_prompt.shape[-1]
    hg_dim = hgrn_lb_logits.shape[-1]
    hg_heads = hg_dim // HG_HEAD_DIM
    q_lora = q_lat_norm.shape[-1]
    kv_lora = kv_lat_norm.shape[-1]
    mla_heads = w_uq.shape[-1] // QK_HEAD

    trunks = []
    row_off = 0
    blocks, info = [], []
    for x in (x_prompt, x_sample):
        bsz, s, _ = x.shape
        lp = s + SEQ_PAD
        meta = jnp.broadcast_to(meta_tokens.astype(x.dtype)[None], (bsz, N_META, d_model))
        pad = jnp.zeros((bsz, lp - s - N_META, d_model), x.dtype)
        blocks.append(jnp.concatenate([meta, x, pad], axis=1).reshape(bsz * lp, d_model))
        nc = lp // HG_CHUNK_ROWS
        for _ in range(bsz):
            for c in range(nc):
                info += [c, nc, s + N_META]
        trunks.append((bsz, s, lp, row_off))
        row_off += bsz * lp
    h = jnp.concatenate(blocks, axis=0).astype(F32)
    info = jnp.asarray(info, jnp.int32)
    tabs = [jnp.tile(_rope_table(lp), (bsz, 1)) for (bsz, s, lp, _) in trunks]

    sizes = (hg_dim,) * 5 + (q_lora, kv_lora, QK_ROPE, d_model, d_model)
    offs = [0]
    for sz in sizes:
        offs.append(offs[-1] + sz)

    for l in range(depth):
        ffn1 = _ffn_weights(w_ffn1_in[l], w_ffn1_out[l])
        ffn2 = _ffn_weights(w_ffn2_in[l], w_ffn2_out[l])
        wl = w_in[l]
        w_hg = wl[:, :offs[5]].astype(BF16)
        w_kr = wl[:, offs[7]:offs[8]]
        w_lat = jnp.concatenate([wl[:, offs[5]:offs[7]], w_kr, _swap_halves(w_kr, 1)], axis=1).astype(BF16)
        w_gh = wl[:, offs[8]:offs[9]].astype(BF16)
        w_ga = wl[:, offs[9]:offs[10]].astype(BF16)
        wq = w_uq[l].reshape(q_lora, mla_heads, QK_HEAD)
        wq = jnp.concatenate([wq, _swap_halves(wq[:, :, QK_NOPE:], 2)], axis=2)
        wq = wq.transpose(1, 0, 2).astype(BF16)
        wkv = w_ukv[l].reshape(kv_lora, mla_heads, QK_NOPE + V_HEAD).transpose(1, 0, 2).astype(BF16)

        h = _ffn(h, norm_ffn1[l], ffn1)

        u = _rmsnorm(h, norm_mix[l])
        hg = _matmul(u, w_hg, head_major=True)
        lat = _matmul(u, w_lat)
        o_h = _hgrn(hg, info, hgrn_lb_logits, hg_out_norm[l], l, hg_heads)

        o_a = []
        for (bsz, s, lp, off), tab in zip(trunks, tabs):
            q, k, v = _qkv_proj(lat, q_lat_norm[l].reshape(1, q_lora).astype(F32),
                                kv_lat_norm[l].reshape(1, kv_lora).astype(F32), wq, wkv,
                                _head_gain(q_head_norm[l]), _head_gain(k_head_norm[l]),
                                tab, off, bsz * lp, q_lora, kv_lora)
            o_a.append(_attention(q, k, v, bsz, lp, s + N_META))
        o_a = jnp.concatenate(o_a, axis=0)

        merged = _merge(u, o_h, o_a, w_gh, w_ga, w_branch_hgrn[l].astype(BF16), w_branch_mla[l].astype(BF16))
        h = _matmul_residual(merged, w_out[l].astype(BF16), h, 1.0)

        h = _ffn(h, norm_ffn2[l], ffn2)

    outs = []
    for (bsz, s, lp, off) in trunks:
        outs.append(h[off:off + bsz * lp].reshape(bsz, lp, d_model)[:, N_META:N_META + s])
    return tuple(outs)
```

```python
import functools
import math

import jax
import jax.numpy as jnp
from jax import lax
from jax.experimental import pallas as pl
from jax.experimental.pallas import tpu as pltpu

N_META = 16
HG_HEAD_DIM = 128
QK_NOPE = 128
QK_ROPE = 64
QK_HEAD = QK_NOPE + QK_ROPE
V_HEAD = 128
ROPE_THETA = 10000.0
RMS_EPS = 1e-6

LANES = 128
SEQ_PAD = 128
HG_CHUNK_ROWS = 128
HG_SUB_ROWS = 32
HG_EXACT_SUB_ROWS = 16
HG_MAX_LOG2_GROWTH = 115.0
ATTN_MAX_Q_ROWS = 640
ATTN_KEY_BLOCKS = 3
QK_PAD = 256
VMEM_LIMIT = 56 * 1024 * 1024

F32 = jnp.float32
BF16 = jnp.bfloat16
NEG_BIG = -1e30
LN2 = math.log(2.0)
LOG2E = 1.0 / LN2


def _cparams(sem):
    return pltpu.CompilerParams(dimension_semantics=sem, vmem_limit_bytes=VMEM_LIMIT)


def _pick(n, prefs):
    for p in prefs:
        if n % p == 0:
            return p
    return n


def _sigmoid(x):
    return 1.0 / (1.0 + jnp.exp(-x))


def _rmsnorm_kernel(x_ref, w_ref, o_ref):
    x = x_ref[...]
    ms = jnp.mean(x * x, axis=-1, keepdims=True)
    o_ref[...] = ((x * lax.rsqrt(ms + RMS_EPS)) * w_ref[...]).astype(o_ref.dtype)


def _rmsnorm(x, w):
    m, d = x.shape
    tm = _pick(m, (512, 256, 128, 64, 32, 16, 8))
    return pl.pallas_call(
        _rmsnorm_kernel,
        grid=(m // tm,),
        in_specs=[pl.BlockSpec((tm, d), lambda i: (i, 0)),
                  pl.BlockSpec((1, d), lambda i: (0, 0))],
        out_specs=pl.BlockSpec((tm, d), lambda i: (i, 0)),
        out_shape=jax.ShapeDtypeStruct((m, d), BF16),
        compiler_params=_cparams(("parallel",)),
        name="rmsnorm",
    )(x, w.reshape(1, d).astype(F32))


def _mm_kernel(a_ref, b_ref, o_ref):
    o_ref[...] = jnp.dot(a_ref[...], b_ref[...], preferred_element_type=F32)


def _mm_headmajor_kernel(a_ref, b_ref, o_ref):
    acc = jnp.dot(a_ref[...], b_ref[...], preferred_element_type=F32)
    for j in range(o_ref.shape[0]):
        o_ref[j] = acc[:, j * LANES:(j + 1) * LANES]


def _matmul(a, b, *, head_major=False):
    m, k = a.shape
    n = b.shape[1]
    if n % 256 == 0:
        tm = _pick(m, (1024, 512, 256, 128, 64, 32, 16))
        tn = _pick(n, (512, 256))
    else:
        tm = _pick(m, (512, 256, 128, 64, 32, 16))
        tn = n
    grid = (m // tm, n // tn)
    in_specs = [pl.BlockSpec((tm, k), lambda i, j: (i, 0)),
                pl.BlockSpec((k, tn), lambda i, j: (0, j))]
    if head_major:
        return pl.pallas_call(
            _mm_headmajor_kernel, grid=grid, in_specs=in_specs,
            out_specs=pl.BlockSpec((tn // LANES, tm, LANES), lambda i, j: (j, i, 0)),
            out_shape=jax.ShapeDtypeStruct((n // LANES, m, LANES), F32),
            compiler_params=_cparams(("parallel", "arbitrary")),
            name="proj_headmajor",
        )(a, b)
    return pl.pallas_call(
        _mm_kernel, grid=grid, in_specs=in_specs,
        out_specs=pl.BlockSpec((tm, tn), lambda i, j: (i, j)),
        out_shape=jax.ShapeDtypeStruct((m, n), F32),
        compiler_params=_cparams(("parallel", "arbitrary")),
        name="proj",
    )(a, b)


def _ffn_in_kernel(x_ref, wg_ref, wu_ref, o_ref):
    x = x_ref[...]
    g = jnp.dot(x, wg_ref[...], preferred_element_type=F32)
    u = jnp.dot(x, wu_ref[...], preferred_element_type=F32)
    o_ref[...] = ((g * _sigmoid(g)) * u).astype(o_ref.dtype)


def _ffn_in(x, wg, wu):
    m, k = x.shape
    f = wg.shape[1]
    tm = _pick(m, (1024, 512, 256, 128, 64, 32, 16))
    tf = min(512, f)
    return pl.pallas_call(
        _ffn_in_kernel,
        grid=(m // tm, pl.cdiv(f, tf)),
        in_specs=[pl.BlockSpec((tm, k), lambda i, j: (i, 0)),
                  pl.BlockSpec((k, tf), lambda i, j: (0, j)),
                  pl.BlockSpec((k, tf), lambda i, j: (0, j))],
        out_specs=pl.BlockSpec((tm, tf), lambda i, j: (i, j)),
        out_shape=jax.ShapeDtypeStruct((m, f), BF16),
        compiler_params=_cparams(("parallel", "arbitrary")),
        name="ffn_in",
    )(x, wg, wu)


def _mm_res_kernel(a_ref, b_ref, r_ref, o_ref, *, scale, k_tail):
    kk = pl.program_id(2)

    @pl.when(kk == 0)
    def _():
        o_ref[...] = r_ref[...]

    a, b = a_ref[...], b_ref[...]
    if k_tail:
        lim = jnp.where(kk == pl.num_programs(2) - 1, k_tail, a.shape[1])
        a = jnp.where(lax.broadcasted_iota(jnp.int32, a.shape, 1) < lim, a, jnp.zeros_like(a))
        b = jnp.where(lax.broadcasted_iota(jnp.int32, b.shape, 0) < lim, b, jnp.zeros_like(b))
    o_ref[...] += scale * jnp.dot(a, b, preferred_element_type=F32)


def _matmul_residual(a, b, res, scale):
    m, k = a.shape
    n = b.shape[1]
    tm = _pick(m, (1024, 512, 256, 128, 64, 32, 16))
    tn = _pick(n, (2048, 1024, 512, 256, 128))
    tk = min(1024, k)
    return pl.pallas_call(
        functools.partial(_mm_res_kernel, scale=scale, k_tail=k % tk),
        grid=(m // tm, n // tn, pl.cdiv(k, tk)),
        in_specs=[pl.BlockSpec((tm, tk), lambda i, j, kk: (i, kk)),
                  pl.BlockSpec((tk, tn), lambda i, j, kk: (kk, j)),
                  pl.BlockSpec((tm, tn), lambda i, j, kk: (i, j))],
        out_specs=pl.BlockSpec((tm, tn), lambda i, j, kk: (i, j)),
        out_shape=jax.ShapeDtypeStruct((m, n), F32),
        compiler_params=_cparams(("parallel", "parallel", "arbitrary")),
        name="proj_residual",
    )(a, b, res)


def _merge_kernel(u_ref, oh_ref, oa_ref, wgh_ref, wga_ref, wbh_ref, wba_ref, o_ref):
    u = u_ref[...]
    oh = jnp.concatenate([oh_ref[h] for h in range(oh_ref.shape[0])], axis=1)
    gh = jnp.dot(u, wgh_ref[...], preferred_element_type=F32)
    ga = jnp.dot(u, wga_ref[...], preferred_element_type=F32)
    bh = jnp.dot(oh, wbh_ref[...], preferred_element_type=F32)
    ba = jnp.dot(oa_ref[...], wba_ref[...], preferred_element_type=F32)
    o_ref[...] = (_sigmoid(gh) * bh + _sigmoid(ga) * ba).astype(o_ref.dtype)


def _merge(u, oh, oa, wgh, wga, wbh, wba):
    m, d = u.shape
    nh = oh.shape[0]
    n = wgh.shape[1]
    tm = _pick(m, (1024, 512, 256, 128, 64, 32, 16))
    tn = _pick(n, (256, 128))
    return pl.pallas_call(
        _merge_kernel,
        grid=(m // tm, n // tn),
        in_specs=[pl.BlockSpec((tm, d), lambda i, j: (i, 0)),
                  pl.BlockSpec((nh, tm, LANES), lambda i, j: (0, i, 0)),
                  pl.BlockSpec((tm, oa.shape[1]), lambda i, j: (i, 0)),
                  pl.BlockSpec((d, tn), lambda i, j: (0, j)),
                  pl.BlockSpec((d, tn), lambda i, j: (0, j)),
                  pl.BlockSpec((wbh.shape[0], tn), lambda i, j: (0, j)),
                  pl.BlockSpec((wba.shape[0], tn), lambda i, j: (0, j))],
        out_specs=pl.BlockSpec((tm, tn), lambda i, j: (i, j)),
        out_shape=jax.ShapeDtypeStruct((m, n), BF16),
        compiler_params=_cparams(("parallel", "arbitrary")),
        name="branch_merge",
    )(u, oh, oa, wgh, wga, wbh, wba)


def _rope_tail(y2, tab):
    w = y2 * tab
    rot = w + pltpu.roll(w, QK_ROPE, axis=1)
    lane = lax.broadcasted_iota(jnp.int32, rot.shape, 1)
    return jnp.where(lane < QK_ROPE, rot, 0.0)


def _rope_sumsq(y2):
    lane = lax.broadcasted_iota(jnp.int32, y2.shape, 1)
    y2m = jnp.where(lane < QK_ROPE, y2, 0.0)
    return jnp.sum(y2m * y2m, axis=-1, keepdims=True)


def _head_rms_scale(y, ones_mask, extra_ss):
    ss = jnp.dot((y * y).astype(BF16), ones_mask, preferred_element_type=F32)
    return lax.rsqrt((ss + extra_ss) * (1.0 / QK_HEAD) + RMS_EPS)


def _latent_norm(x, w):
    ms = jnp.mean(x * x, axis=-1, keepdims=True)
    return ((x * lax.rsqrt(ms + RMS_EPS)) * w).astype(BF16)


def _qkv_kernel(lat_ref, qlw_ref, kvlw_ref, wq_ref, wkv_ref, gq_ref, gk_ref, tab_ref,
                q_ref, k_ref, v_ref, *, q_lora, kv_lora, q_scale):
    xq = _latent_norm(lat_ref[:, 0:q_lora], qlw_ref[...])
    xkv = _latent_norm(lat_ref[:, q_lora:q_lora + kv_lora], kvlw_ref[...])
    kr = lat_ref[:, q_lora + kv_lora:q_lora + kv_lora + LANES]
    gq1 = gq_ref[:, :QK_NOPE] * q_scale
    gk1 = gk_ref[:, :QK_NOPE]
    tab_q = tab_ref[...] * (gq_ref[:, QK_NOPE:] * q_scale)
    tab_k = tab_ref[...] * gk_ref[:, QK_NOPE:]
    kr_ss = _rope_sumsq(kr)
    row = lax.broadcasted_iota(jnp.int32, (QK_PAD, LANES), 0)
    ones_q = jnp.where(row < QK_HEAD, 1.0, 0.0).astype(BF16)
    ones_k = jnp.ones((QK_NOPE, LANES), BF16)

    def body(h, carry):
        y = jnp.dot(xq, wq_ref[h], preferred_element_type=F32)
        y1, y2 = y[:, :QK_NOPE], y[:, QK_NOPE:]
        r = _head_rms_scale(y, ones_q, 0.0)
        q_ref[h, :, 0:QK_NOPE] = ((y1 * r) * gq1).astype(BF16)
        q_ref[h, :, QK_NOPE:] = _rope_tail(y2 * r, tab_q).astype(BF16)

        z = jnp.dot(xkv, wkv_ref[h], preferred_element_type=F32)
        z1 = z[:, :QK_NOPE]
        rk = _head_rms_scale(z1, ones_k, kr_ss)
        k_ref[h, :, 0:QK_NOPE] = ((z1 * rk) * gk1).astype(BF16)
        k_ref[h, :, QK_NOPE:] = _rope_tail(kr * rk, tab_k).astype(BF16)
        v_ref[h] = z[:, QK_NOPE:].astype(BF16)
        return carry

    lax.fori_loop(0, q_ref.shape[0], body, 0, unroll=2)


def _qkv_proj(lat, q_lat_w, kv_lat_w, wq, wkv, gq, gk, tab, row_off, rows, q_lora, kv_lora):
    nh = wq.shape[0]
    tl = _pick(math.gcd(rows, row_off) if row_off else rows, (512, 256, 128, 64, 32, 16))
    off = row_off // tl
    full = lambda shape: pl.BlockSpec(shape, lambda i: (0,) * len(shape))
    return pl.pallas_call(
        functools.partial(_qkv_kernel, q_lora=q_lora, kv_lora=kv_lora,
                          q_scale=LOG2E / math.sqrt(QK_HEAD)),
        grid=(rows // tl,),
        in_specs=[pl.BlockSpec((tl, lat.shape[1]), lambda i: (off + i, 0)),
                  full((1, q_lora)), full((1, kv_lora)),
                  full((nh, q_lora, QK_PAD)), full((nh, kv_lora, QK_NOPE + V_HEAD)),
                  full((1, QK_PAD)), full((1, QK_PAD)),
                  pl.BlockSpec((tl, LANES), lambda i: (i, 0))],
        out_specs=[pl.BlockSpec((nh, tl, QK_PAD), lambda i: (0, i, 0)),
                   pl.BlockSpec((nh, tl, QK_PAD), lambda i: (0, i, 0)),
                   pl.BlockSpec((nh, tl, V_HEAD), lambda i: (0, i, 0))],
        out_shape=[jax.ShapeDtypeStruct((nh, rows, QK_PAD), BF16),
                   jax.ShapeDtypeStruct((nh, rows, QK_PAD), BF16),
                   jax.ShapeDtypeStruct((nh, rows, V_HEAD), BF16)],
        compiler_params=_cparams(("parallel",)),
        name="mla_qkv_proj",
    )(lat, q_lat_w, kv_lat_w, wq, wkv, gq, gk, tab)


def _attn_kernel(q_ref, k_ref, v_ref, o_ref, *, n_valid, n_sub):
    rows = q_ref.shape[1] // n_sub
    for r in range(n_sub):
        sl = slice(r * rows, (r + 1) * rows)
        o_ref[sl, :] = _attn_rows(q_ref[0, sl, :], k_ref, v_ref, n_valid).astype(o_ref.dtype)


def _attn_rows(q, k_ref, v_ref, n_valid):
    lp = k_ref.shape[1]
    kb = -(-n_valid // (ATTN_KEY_BLOCKS * QK_PAD)) * QK_PAD
    edges = list(range(0, min(lp, n_valid), kb)) + [min(lp, -(-n_valid // LANES) * LANES)]
    m = denom = acc = None
    for c0, c1 in zip(edges[:-1], edges[1:]):
        s = lax.dot_general(q, k_ref[0, c0:c1, :], (((1,), (1,)), ((), ())), preferred_element_type=F32)
        if c1 > n_valid:
            col = lax.broadcasted_iota(jnp.int32, s.shape, 1) + c0
            s = jnp.where(col < n_valid, s, NEG_BIG)
        mb = jnp.max(s, axis=-1, keepdims=True)
        if m is None:
            m = mb
            p = jnp.exp2(s - m)
            denom = jnp.sum(p, axis=-1, keepdims=True)
            acc = jnp.dot(p.astype(BF16), v_ref[0, c0:c1, :], preferred_element_type=F32)
        else:
            m_new = jnp.maximum(m, mb)
            alpha = jnp.exp2(m - m_new)
            p = jnp.exp2(s - m_new)
            denom = alpha * denom + jnp.sum(p, axis=-1, keepdims=True)
            acc = alpha * acc + jnp.dot(p.astype(BF16), v_ref[0, c0:c1, :], preferred_element_type=F32)
            m = m_new
    return acc / denom


def _attention(q, k, v, batch, lp, n_valid):
    nh = q.shape[0]
    nq = next(n for n in range(1, lp + 1) if lp % n == 0 and (lp // n) % 16 == 0 and lp // n <= ATTN_MAX_Q_ROWS)
    tq = lp // nq
    n_sub = 1
    return pl.pallas_call(
        functools.partial(_attn_kernel, n_valid=n_valid, n_sub=n_sub),
        grid=(batch, nh, nq),
        in_specs=[pl.BlockSpec((1, tq, QK_PAD), lambda b, h, i: (h, b * nq + i, 0)),
                  pl.BlockSpec((1, lp, QK_PAD), lambda b, h, i: (h, b, 0)),
                  pl.BlockSpec((1, lp, V_HEAD), lambda b, h, i: (h, b, 0))],
        out_specs=pl.BlockSpec((tq, V_HEAD), lambda b, h, i: (b * nq + i, h)),
        out_shape=jax.ShapeDtypeStruct((batch * lp, nh * V_HEAD), BF16),
        compiler_params=_cparams(("parallel", "parallel", "arbitrary")),
        name="mla_attention",
    )(q, k, v)


def _lower_bound(logit_ref, h, layer):
    lg = logit_ref[0, h]
    e = jnp.exp(lg - jnp.max(lg, axis=0, keepdims=True))
    p = e / jnp.sum(e, axis=0, keepdims=True)
    lb = jnp.zeros((1, lg.shape[1]), F32)
    for j in range(1, layer + 1):
        lb = lb + p[j:j + 1]
    return lb


def _gates(zq, zf, zi, lb, row_ok):
    q = zq * _sigmoid(zq)
    e = jnp.exp(-jnp.abs(zf))
    r = 1.0 / (1.0 + e)
    er = e * r
    pos = zf >= 0.0
    k = jnp.where(row_ok, (1.0 - lb) * jnp.where(pos, er, r), 0.0)
    f = lb + (1.0 - lb) * jnp.where(pos, r, er)
    g = jnp.maximum(jnp.log(f) * LOG2E, (jnp.log1p(-lb) * LOG2E - 1.0) + jnp.minimum(zf, 0.0) * LOG2E)
    return q, k, zi, g


def _log_cumsum(g, tri):
    g1 = g.astype(BF16)
    r1 = g - g1.astype(F32)
    g2 = r1.astype(BF16)
    g3 = (r1 - g2.astype(F32)).astype(BF16)
    bb = jnp.dot(tri, jnp.concatenate([g1, g2, g3], axis=1), preferred_element_type=F32)
    return (bb[:, :LANES] + bb[:, LANES:2 * LANES]) + bb[:, 2 * LANES:]


def _carried(q, k, v, b, total, st):
    qh = (q * jnp.exp2(b)).astype(BF16)
    kh = (k * jnp.exp2(total - b)).astype(BF16)
    o = lax.dot_general(qh, st.astype(BF16), (((1,), (1,)), ((), ())), preferred_element_type=F32)
    st_new = st * jnp.exp2(total) + lax.dot_general(
        v.astype(BF16), kh, (((0,), (0,)), ((), ())), preferred_element_type=F32)
    return o, st_new


def _hgrn_chunk_factorised(q, k, v, g, st, tri, in_scan, reverse):
    c_rows = q.shape[0]
    sub = HG_SUB_ROWS
    b = _log_cumsum(g, tri)
    total = b[0:1] if reverse else b[c_rows - 1:c_rows]
    o, st_new = _carried(q, k, v, b, total, st)

    zero_row = jnp.zeros((1, LANES), F32)
    growth = zero_row
    rows = []
    for blk in range(c_rows // sub):
        lo, hi = blk * sub, (blk + 1) * sub
        if reverse:
            ref = b[hi:hi + 1] if hi < c_rows else zero_row
            growth = jnp.maximum(growth, ref - b[lo:lo + 1])
        else:
            ref = b[lo - 1:lo] if lo > 0 else zero_row
            growth = jnp.maximum(growth, ref - b[hi - 1:hi])
        qi = (q[lo:hi] * jnp.exp2(b[lo:hi] - ref)).astype(BF16)
        ks = (k * jnp.exp2(jnp.minimum(ref - b, HG_MAX_LOG2_GROWTH))).astype(BF16)
        rows.append(lax.dot_general(qi, ks, (((1,), (1,)), ((), ())), preferred_element_type=F32))
    a = jnp.where(in_scan, jnp.concatenate(rows, axis=0), 0.0)
    o = o + jnp.dot(a.astype(BF16), v.astype(BF16), preferred_element_type=F32)
    return o, st_new, growth


def _hgrn_chunk_exact(q, k, v, g, st, tri, ones_sq, reverse):
    c_rows = q.shape[0]
    sub = HG_EXACT_SUB_ROWS
    nb = c_rows // sub
    b = _log_cumsum(g, tri)
    total = b[0:1] if reverse else b[c_rows - 1:c_rows]
    o, _ = _carried(q, k, v, b, total, st)

    vb = v.astype(BF16)
    pieces = []
    for blk in range(nb):
        lo, hi = blk * sub, (blk + 1) * sub
        if reverse:
            if blk == nb - 1:
                pieces.append(jnp.zeros((sub, LANES), F32))
                continue
            ref = b[hi:hi + 1]
            other = slice(hi, c_rows)
        else:
            if blk == 0:
                pieces.append(jnp.zeros((sub, LANES), F32))
                continue
            ref = b[lo - 1:lo]
            other = slice(0, lo)
        qi = (q[lo:hi] * jnp.exp2(b[lo:hi] - ref)).astype(BF16)
        ko = (k[other] * jnp.exp2(ref - b[other])).astype(BF16)
        a = lax.dot_general(qi, ko, (((1,), (1,)), ((), ())), preferred_element_type=F32)
        pieces.append(jnp.dot(a.astype(BF16), vb[other], preferred_element_type=F32))
    o = o + jnp.concatenate(pieces, axis=0)

    pos = lax.broadcasted_iota(jnp.int32, (c_rows, LANES), 0) % sub
    d_list, v_list = [], []
    for d in range(sub):
        if d == 0:
            vs = v
            dmat = q * k
        else:
            shift = (c_rows - d) if reverse else d
            ks = pltpu.roll(k, shift, axis=0)
            bs = pltpu.roll(b, shift, axis=0)
            vs = pltpu.roll(v, shift, axis=0)
            ok = (pos + d < sub) if reverse else (pos >= d)
            dmat = (q * ks) * jnp.exp2(jnp.where(ok, b - bs, NEG_BIG))
        d_list.append(dmat.astype(BF16))
        v_list.append(vs)
    rs = jnp.dot(jnp.concatenate(d_list, axis=0), ones_sq, preferred_element_type=F32)
    for d in range(sub):
        o = o + rs[d * c_rows:(d + 1) * c_rows] * v_list[d]
    return o


def _hgrn_kernel(info_ref, *refs, layer, reverse):
    if reverse:
        zq_ref, zf_ref, zi_ref, zg_ref, of_ref, lg_ref, nw_ref, o_ref, st_ref = refs
    else:
        zq_ref, zf_ref, zi_ref, lg_ref, o_ref, st_ref = refs
    step = pl.program_id(0)
    g = (pl.num_programs(0) - 1 - step) if reverse else step
    local, n_chunks, seq_rows = info_ref[3 * g], info_ref[3 * g + 1], info_ref[3 * g + 2]
    nh, c_rows = zq_ref.shape[0], zq_ref.shape[1]
    cur = step % 2

    @pl.when((local == n_chunks - 1) if reverse else (local == 0))
    def _():
        st_ref[cur] = jnp.zeros(st_ref.shape[1:], F32)

    rr = lax.broadcasted_iota(jnp.int32, (c_rows, c_rows), 0)
    cc = lax.broadcasted_iota(jnp.int32, (c_rows, c_rows), 1)
    in_scan = (cc >= rr) if reverse else (cc <= rr)
    tri = jnp.where(in_scan, 1.0, 0.0).astype(BF16)
    row_ok = (lax.broadcasted_iota(jnp.int32, (c_rows, LANES), 0) + local * c_rows) < seq_rows

    def finish(h, o):
        if reverse:
            o = o + of_ref[h]
            ms = jnp.mean(o * o, axis=-1, keepdims=True)
            zg = zg_ref[h]
            o_ref[h] = (((o * lax.rsqrt(ms + RMS_EPS)) * nw_ref[...]) * (zg * _sigmoid(zg))).astype(o_ref.dtype)
        else:
            o_ref[h] = o

    def fast_body(h, growth):
        q, k, v, gl = _gates(zq_ref[h], zf_ref[h], zi_ref[h], _lower_bound(lg_ref, h, layer), row_ok)
        o, st_new, gr = _hgrn_chunk_factorised(q, k, v, gl, st_ref[cur, h], tri, in_scan, reverse)
        st_ref[1 - cur, h] = st_new
        finish(h, o)
        return jnp.maximum(growth, gr)

    growth = lax.fori_loop(0, nh, fast_body, jnp.zeros((1, LANES), F32), unroll=16)

    @pl.when(jnp.max(growth) > HG_MAX_LOG2_GROWTH)
    def _():
        ones_sq = jnp.ones((LANES, LANES), BF16)

        def exact_body(h, carry):
            q, k, v, gl = _gates(zq_ref[h], zf_ref[h], zi_ref[h], _lower_bound(lg_ref, h, layer), row_ok)
            finish(h, _hgrn_chunk_exact(q, k, v, gl, st_ref[cur, h], tri, ones_sq, reverse))
            return carry

        lax.fori_loop(0, nh, exact_body, 0)


def _hgrn(hg, info, lb_logits, out_norm, layer, nh):
    m = hg.shape[1]
    c = HG_CHUNK_ROWS
    n_chunks = m // c
    depth = lb_logits.shape[1]
    lg = lb_logits.astype(F32).reshape(2, depth, nh, LANES).transpose(0, 2, 1, 3)

    def blk(grp, rev):
        if rev:
            return pl.BlockSpec((nh, c, LANES), lambda g, info_ref: (grp, n_chunks - 1 - g, 0))
        return pl.BlockSpec((nh, c, LANES), lambda g, info_ref: (grp, g, 0))

    state = pltpu.VMEM((2, nh, LANES, LANES), F32)
    o_f = pl.pallas_call(
        functools.partial(_hgrn_kernel, layer=layer, reverse=False),
        grid_spec=pltpu.PrefetchScalarGridSpec(
            num_scalar_prefetch=1, grid=(n_chunks,),
            in_specs=[blk(0, False), blk(1, False), blk(3, False),
                      pl.BlockSpec((1, nh, depth, LANES), lambda g, info_ref: (0, 0, 0, 0))],
            out_specs=blk(0, False),
            scratch_shapes=[state]),
        out_shape=jax.ShapeDtypeStruct((nh, m, LANES), F32),
        compiler_params=_cparams(("arbitrary",)),
        name="hgrn_forward",
    )(info, hg, hg, hg, lg)

    return pl.pallas_call(
        functools.partial(_hgrn_kernel, layer=layer, reverse=True),
        grid_spec=pltpu.PrefetchScalarGridSpec(
            num_scalar_prefetch=1, grid=(n_chunks,),
            in_specs=[blk(0, True), blk(2, True), blk(3, True), blk(4, True), blk(0, True),
                      pl.BlockSpec((1, nh, depth, LANES), lambda g, info_ref: (1, 0, 0, 0)),
                      pl.BlockSpec((1, LANES), lambda g, info_ref: (0, 0))],
            out_specs=blk(0, True),
            scratch_shapes=[state]),
        out_shape=jax.ShapeDtypeStruct((nh, m, LANES), BF16),
        compiler_params=_cparams(("arbitrary",)),
        name="hgrn_backward",
    )(info, hg, hg, hg, hg, o_f, lg, out_norm.reshape(1, LANES).astype(F32))


def _swap_halves(x, axis):
    a, b = jnp.split(x, 2, axis=axis)
    return jnp.concatenate([b, a], axis=axis)


def _rope_table(rows):
    inv_freq = 1.0 / (ROPE_THETA ** (jnp.arange(0, QK_ROPE, 2, dtype=F32) / QK_ROPE))
    ang = jnp.arange(rows, dtype=F32)[:, None] * inv_freq[None, :]
    c, s = jnp.cos(ang), jnp.sin(ang)
    return jnp.concatenate([c, c, -s, s], axis=1)


def _head_gain(g):
    g = g.astype(F32)
    return jnp.concatenate([g[:QK_NOPE], g[QK_NOPE:], _swap_halves(g[QK_NOPE:], 0)]).reshape(1, QK_PAD)


def _ffn_weights(w_in, w_out):
    d_ff = w_out.shape[0]
    return w_in[:, :d_ff].astype(BF16), w_in[:, d_ff:].astype(BF16), w_out.astype(BF16)


def _ffn(h, norm_w, weights, scale=0.5):
    wg, wu, wo = weights
    return _matmul_residual(_ffn_in(_rmsnorm(h, norm_w), wg, wu), wo, h, scale)


def kernel(x_prompt, x_sample, meta_tokens, hgrn_lb_logits, norm_ffn1, w_ffn1_in, w_ffn1_out, norm_mix, w_in,
           q_lat_norm, w_uq, kv_lat_norm, w_ukv, q_head_norm, k_head_norm, hg_out_norm,
           w_branch_hgrn, w_branch_mla, w_out, norm_ffn2, w_ffn2_in, w_ffn2_out):
    depth = norm_ffn1.shape[0]
    d_model = x_prompt.shape[-1]
    hg_dim = hgrn_lb_logits.shape[-1]
    hg_heads = hg_dim // HG_HEAD_DIM
    q_lora = q_lat_norm.shape[-1]
    kv_lora = kv_lat_norm.shape[-1]
    mla_heads = w_uq.shape[-1] // QK_HEAD

    trunks = []
    row_off = 0
    blocks, info = [], []
    for x in (x_prompt, x_sample):
        bsz, s, _ = x.shape
        lp = s + SEQ_PAD
        meta = jnp.broadcast_to(meta_tokens.astype(x.dtype)[None], (bsz, N_META, d_model))
        pad = jnp.zeros((bsz, lp - s - N_META, d_model), x.dtype)
        blocks.append(jnp.concatenate([meta, x, pad], axis=1).reshape(bsz * lp, d_model))
        nc = lp // HG_CHUNK_ROWS
        for _ in range(bsz):
            for c in range(nc):
                info += [c, nc, s + N_META]
        trunks.append((bsz, s, lp, row_off))
        row_off += bsz * lp
    h = jnp.concatenate(blocks, axis=0).astype(F32)
    info = jnp.asarray(info, jnp.int32)
    tabs = [jnp.tile(_rope_table(lp), (bsz, 1)) for (bsz, s, lp, _) in trunks]

    sizes = (hg_dim,) * 5 + (q_lora, kv_lora, QK_ROPE, d_model, d_model)
    offs = [0]
    for sz in sizes:
        offs.append(offs[-1] + sz)

    for l in range(depth):
        ffn1 = _ffn_weights(w_ffn1_in[l], w_ffn1_out[l])
        ffn2 = _ffn_weights(w_ffn2_in[l], w_ffn2_out[l])
        wl = w_in[l]
        w_hg = wl[:, :offs[5]].astype(BF16)
        w_kr = wl[:, offs[7]:offs[8]]
        w_lat = jnp.concatenate([wl[:, offs[5]:offs[7]], w_kr, _swap_halves(w_kr, 1)], axis=1).astype(BF16)
        w_gh = wl[:, offs[8]:offs[9]].astype(BF16)
        w_ga = wl[:, offs[9]:offs[10]].astype(BF16)
        wq = w_uq[l].reshape(q_lora, mla_heads, QK_HEAD)
        wq = jnp.concatenate([wq, _swap_halves(wq[:, :, QK_NOPE:], 2)], axis=2)
        wq = wq.transpose(1, 0, 2).astype(BF16)
        wkv = w_ukv[l].reshape(kv_lora, mla_heads, QK_NOPE + V_HEAD).transpose(1, 0, 2).astype(BF16)

        h = _ffn(h, norm_ffn1[l], ffn1)

        u = _rmsnorm(h, norm_mix[l])
        hg = _matmul(u, w_hg, head_major=True)
        lat = _matmul(u, w_lat)
        o_h = _hgrn(hg, info, hgrn_lb_logits, hg_out_norm[l], l, hg_heads)

        o_a = []
        for (bsz, s, lp, off), tab in zip(trunks, tabs):
            q, k, v = _qkv_proj(lat, q_lat_norm[l].reshape(1, q_lora).astype(F32),
                                kv_lat_norm[l].reshape(1, kv_lora).astype(F32), wq, wkv,
                                _head_gain(q_head_norm[l]), _head_gain(k_head_norm[l]),
                                tab, off, bsz * lp, q_lora, kv_lora)
            o_a.append(_attention(q, k, v, bsz, lp, s + N_META))
        o_a = jnp.concatenate(o_a, axis=0)

        merged = _merge(u, o_h, o_a, w_gh, w_ga, w_branch_hgrn[l].astype(BF16), w_branch_mla[l].astype(BF16))
        h = _matmul_residual(merged, w_out[l].astype(BF16), h, 1.0)

        h = _ffn(h, norm_ffn2[l], ffn2)

    outs = []
    for (bsz, s, lp, off) in trunks:
        outs.append(h[off:off + bsz * lp].reshape(bsz, lp, d_model)[:, N_META:N_META + s])
    return tuple(outs)
```

```python
import functools
import math

import jax
import jax.numpy as jnp
from jax import lax
from jax.experimental import pallas as pl
from jax.experimental.pallas import tpu as pltpu

N_META = 16
HG_HEAD_DIM = 128
QK_NOPE = 128
QK_ROPE = 64
QK_HEAD = QK_NOPE + QK_ROPE
V_HEAD = 128
ROPE_THETA = 10000.0
RMS_EPS = 1e-6

LANES = 128
SEQ_PAD = 128
HG_CHUNK_ROWS = 128
HG_SUB_ROWS = 32
HG_EXACT_SUB_ROWS = 16
HG_MAX_LOG2_GROWTH = 115.0
ATTN_MAX_Q_ROWS = 640
ATTN_KEY_BLOCKS = 3
QK_PAD = 256
VMEM_LIMIT = 56 * 1024 * 1024
VMEM_LIMIT_LARGE = 58 * 1024 * 1024
F32 = jnp.float32
BF16 = jnp.bfloat16
NEG_BIG = -1e30
LN2 = math.log(2.0)
LOG2E = 1.0 / LN2


def _cparams(sem, vmem_limit=VMEM_LIMIT):
    return pltpu.CompilerParams(dimension_semantics=sem, vmem_limit_bytes=vmem_limit)


def _pick(n, prefs):
    for p in prefs:
        if n % p == 0:
            return p
    return n


def _sigmoid(x):
    return 1.0 / (1.0 + jnp.exp(-x))


def _row_scale_kernel(x_ref, hb_ref, r_ref):
    x = x_ref[...]
    hb_ref[...] = x.astype(BF16)
    r_ref[...] = lax.rsqrt(jnp.mean(x * x, axis=-1, keepdims=True) + RMS_EPS)


def _row_scale(x):
    m, d = x.shape
    tm = _pick(m, (512, 256, 128, 64, 32, 16, 8))
    return pl.pallas_call(
        _row_scale_kernel,
        grid=(m // tm,),
        in_specs=[pl.BlockSpec((tm, d), lambda i: (i, 0))],
        out_specs=[pl.BlockSpec((tm, d), lambda i: (i, 0)),
                   pl.BlockSpec((tm, 1), lambda i: (i, 0))],
        out_shape=[jax.ShapeDtypeStruct((m, d), BF16), jax.ShapeDtypeStruct((m, 1), F32)],
        compiler_params=_cparams(("parallel",)),
        name="row_scale",
    )(x)


def _mm_kernel(a_ref, r_ref, b_ref, o_ref):
    o_ref[...] = jnp.dot(a_ref[...], b_ref[...], preferred_element_type=F32) * r_ref[...]


def _mm_headmajor_kernel(a_ref, r_ref, b_ref, o_ref):
    acc = jnp.dot(a_ref[...], b_ref[...], preferred_element_type=F32) * r_ref[...]
    for j in range(o_ref.shape[0]):
        o_ref[j] = acc[:, j * LANES:(j + 1) * LANES]


def _matmul(a, r, b, *, head_major=False):
    m, k = a.shape
    n = b.shape[1]
    if n % 256 == 0:
        tm = _pick(m, (1024, 512, 256, 128, 64, 32, 16))
        tn = _pick(n, (512, 256))
    else:
        tm = _pick(m, (512, 256, 128, 64, 32, 16))
        tn = n
    grid = (m // tm, n // tn)
    in_specs = [pl.BlockSpec((tm, k), lambda i, j: (i, 0)),
                pl.BlockSpec((tm, 1), lambda i, j: (i, 0)),
                pl.BlockSpec((k, tn), lambda i, j: (0, j))]
    if head_major:
        return pl.pallas_call(
            _mm_headmajor_kernel, grid=grid, in_specs=in_specs,
            out_specs=pl.BlockSpec((tn // LANES, tm, LANES), lambda i, j: (j, i, 0)),
            out_shape=jax.ShapeDtypeStruct((n // LANES, m, LANES), F32),
            compiler_params=_cparams(("parallel", "arbitrary")),
            name="proj_headmajor",
        )(a, r, b)
    return pl.pallas_call(
        _mm_kernel, grid=grid, in_specs=in_specs,
        out_specs=pl.BlockSpec((tm, tn), lambda i, j: (i, j)),
        out_shape=jax.ShapeDtypeStruct((m, n), F32),
        compiler_params=_cparams(("parallel", "arbitrary")),
        name="proj",
    )(a, r, b)


def _ffn_in_kernel(x_ref, r_ref, wg_ref, wu_ref, o_ref):
    x = x_ref[...]
    r = r_ref[...]
    g = jnp.dot(x, wg_ref[...], preferred_element_type=F32) * r
    u = jnp.dot(x, wu_ref[...], preferred_element_type=F32) * r
    o_ref[...] = ((g * _sigmoid(g)) * u).astype(o_ref.dtype)


def _ffn_in(x, r, wg, wu):
    m, k = x.shape
    f = wg.shape[1]
    tm = _pick(m, (1024, 512, 256, 128, 64, 32, 16))
    tf = min(512, f)
    return pl.pallas_call(
        _ffn_in_kernel,
        grid=(m // tm, pl.cdiv(f, tf)),
        in_specs=[pl.BlockSpec((tm, k), lambda i, j: (i, 0)),
                  pl.BlockSpec((tm, 1), lambda i, j: (i, 0)),
                  pl.BlockSpec((k, tf), lambda i, j: (0, j)),
                  pl.BlockSpec((k, tf), lambda i, j: (0, j))],
        out_specs=pl.BlockSpec((tm, tf), lambda i, j: (i, j)),
        out_shape=jax.ShapeDtypeStruct((m, f), BF16),
        compiler_params=_cparams(("parallel", "arbitrary")),
        name="ffn_in",
    )(x, r, wg, wu)


def _mm_res_kernel(a_ref, b_ref, res_ref, o_ref, hb_ref, rs_ref, ss_ref, *, scale, k_tail):
    j, kk = pl.program_id(1), pl.program_id(2)

    @pl.when(kk == 0)
    def _():
        o_ref[...] = res_ref[...]

    a, b = a_ref[...], b_ref[...]
    if k_tail:
        lim = jnp.where(kk == pl.num_programs(2) - 1, k_tail, a.shape[1])
        a = jnp.where(lax.broadcasted_iota(jnp.int32, a.shape, 1) < lim, a, jnp.zeros_like(a))
        b = jnp.where(lax.broadcasted_iota(jnp.int32, b.shape, 0) < lim, b, jnp.zeros_like(b))
    o_ref[...] += scale * jnp.dot(a, b, preferred_element_type=F32)

    @pl.when(kk == pl.num_programs(2) - 1)
    def _():
        o = o_ref[...]
        hb_ref[...] = o.astype(BF16)
        part = jnp.sum(o * o, axis=-1, keepdims=True)
        ss = jnp.where(j == 0, part, ss_ref[...] + part)
        ss_ref[...] = ss
        d_model = o.shape[1] * pl.num_programs(1)
        rs_ref[...] = lax.rsqrt(ss * (1.0 / d_model) + RMS_EPS)


def _matmul_residual(a, b, res, scale):
    m, k = a.shape
    n = b.shape[1]
    tm = _pick(m, (1024, 512, 256, 128, 64, 32, 16))
    tn = _pick(n, (2048, 1024, 512, 256, 128))
    tk = min(1024, k)
    return pl.pallas_call(
        functools.partial(_mm_res_kernel, scale=scale, k_tail=k % tk),
        grid=(m // tm, n // tn, pl.cdiv(k, tk)),
        in_specs=[pl.BlockSpec((tm, tk), lambda i, j, kk: (i, kk)),
                  pl.BlockSpec((tk, tn), lambda i, j, kk: (kk, j)),
                  pl.BlockSpec((tm, tn), lambda i, j, kk: (i, j))],
        out_specs=[pl.BlockSpec((tm, tn), lambda i, j, kk: (i, j)),
                   pl.BlockSpec((tm, tn), lambda i, j, kk: (i, j)),
                   pl.BlockSpec((tm, 1), lambda i, j, kk: (i, 0))],
        out_shape=[jax.ShapeDtypeStruct((m, n), F32), jax.ShapeDtypeStruct((m, n), BF16),
                   jax.ShapeDtypeStruct((m, 1), F32)],
        scratch_shapes=[pltpu.VMEM((tm, 1), F32)],
        compiler_params=_cparams(("parallel", "arbitrary", "arbitrary"), VMEM_LIMIT_LARGE),
        name="proj_residual",
    )(a, b, res)


def _merge_kernel(u_ref, r_ref, oh_ref, oa_ref, wgh_ref, wga_ref, wbh_ref, wba_ref, o_ref):
    u = u_ref[...]
    r = r_ref[...]
    oh = jnp.concatenate([oh_ref[h] for h in range(oh_ref.shape[0])], axis=1)
    gh = jnp.dot(u, wgh_ref[...], preferred_element_type=F32) * r
    ga = jnp.dot(u, wga_ref[...], preferred_element_type=F32) * r
    bh = jnp.dot(oh, wbh_ref[...], preferred_element_type=F32)
    ba = jnp.dot(oa_ref[...], wba_ref[...], preferred_element_type=F32)
    o_ref[...] = (_sigmoid(gh) * bh + _sigmoid(ga) * ba).astype(o_ref.dtype)


def _merge(u, r, oh, oa, wgh, wga, wbh, wba):
    m, d = u.shape
    nh = oh.shape[0]
    n = wgh.shape[1]
    tm = _pick(m, (1024, 512, 256, 128, 64, 32, 16))
    tn = _pick(n, (256, 128))
    return pl.pallas_call(
        _merge_kernel,
        grid=(m // tm, n // tn),
        in_specs=[pl.BlockSpec((tm, d), lambda i, j: (i, 0)),
                  pl.BlockSpec((tm, 1), lambda i, j: (i, 0)),
                  pl.BlockSpec((nh, tm, LANES), lambda i, j: (0, i, 0)),
                  pl.BlockSpec((tm, oa.shape[1]), lambda i, j: (i, 0)),
                  pl.BlockSpec((d, tn), lambda i, j: (0, j)),
                  pl.BlockSpec((d, tn), lambda i, j: (0, j)),
                  pl.BlockSpec((wbh.shape[0], tn), lambda i, j: (0, j)),
                  pl.BlockSpec((wba.shape[0], tn), lambda i, j: (0, j))],
        out_specs=pl.BlockSpec((tm, tn), lambda i, j: (i, j)),
        out_shape=jax.ShapeDtypeStruct((m, n), BF16),
        compiler_params=_cparams(("parallel", "arbitrary")),
        name="branch_merge",
    )(u, r, oh, oa, wgh, wga, wbh, wba)


def _rope_tail(y2, tab):
    w = y2 * tab
    rot = w + pltpu.roll(w, QK_ROPE, axis=1)
    lane = lax.broadcasted_iota(jnp.int32, rot.shape, 1)
    return jnp.where(lane < QK_ROPE, rot, 0.0)


def _rope_sumsq(y2):
    lane = lax.broadcasted_iota(jnp.int32, y2.shape, 1)
    y2m = jnp.where(lane < QK_ROPE, y2, 0.0)
    return jnp.sum(y2m * y2m, axis=-1, keepdims=True)


def _head_rms_scale(y, ones_mask, extra_ss):
    ss = jnp.dot((y * y).astype(BF16), ones_mask, preferred_element_type=F32)
    return lax.rsqrt((ss + extra_ss) * (1.0 / QK_HEAD) + RMS_EPS)


def _latent_norm(x, w):
    ms = jnp.mean(x * x, axis=-1, keepdims=True)
    return ((x * lax.rsqrt(ms + RMS_EPS)) * w).astype(BF16)


def _qkv_kernel(lat_ref, qlw_ref, kvlw_ref, wq_ref, wkv_ref, gq_ref, gk_ref, tab_ref,
                q_ref, k_ref, v_ref, *, q_lora, kv_lora, q_scale):
    xq = _latent_norm(lat_ref[:, 0:q_lora], qlw_ref[...])
    xkv = _latent_norm(lat_ref[:, q_lora:q_lora + kv_lora], kvlw_ref[...])
    kr = lat_ref[:, q_lora + kv_lora:q_lora + kv_lora + LANES]
    gq1 = gq_ref[:, :QK_NOPE] * q_scale
    gk1 = gk_ref[:, :QK_NOPE]
    tab_q = tab_ref[...] * (gq_ref[:, QK_NOPE:] * q_scale)
    tab_k = tab_ref[...] * gk_ref[:, QK_NOPE:]
    kr_ss = _rope_sumsq(kr)
    row = lax.broadcasted_iota(jnp.int32, (QK_PAD, LANES), 0)
    ones_q = jnp.where(row < QK_HEAD, 1.0, 0.0).astype(BF16)
    ones_k = jnp.ones((QK_NOPE, LANES), BF16)

    def body(h, carry):
        y = jnp.dot(xq, wq_ref[h], preferred_element_type=F32)
        y1, y2 = y[:, :QK_NOPE], y[:, QK_NOPE:]
        r = _head_rms_scale(y, ones_q, 0.0)
        q_ref[h, :, 0:QK_NOPE] = ((y1 * r) * gq1).astype(BF16)
        q_ref[h, :, QK_NOPE:] = _rope_tail(y2 * r, tab_q).astype(BF16)

        z = jnp.dot(xkv, wkv_ref[h], preferred_element_type=F32)
        z1 = z[:, :QK_NOPE]
        rk = _head_rms_scale(z1, ones_k, kr_ss)
        k_ref[h, :, 0:QK_NOPE] = ((z1 * rk) * gk1).astype(BF16)
        k_ref[h, :, QK_NOPE:] = _rope_tail(kr * rk, tab_k).astype(BF16)
        v_ref[h] = z[:, QK_NOPE:].astype(BF16)
        return carry

    lax.fori_loop(0, q_ref.shape[0], body, 0, unroll=2)


def _qkv_proj(lat, q_lat_w, kv_lat_w, wq, wkv, gq, gk, tab, row_off, rows, q_lora, kv_lora):
    nh = wq.shape[0]
    tl = _pick(math.gcd(rows, row_off) if row_off else rows, (512, 256, 128, 64, 32, 16))
    off = row_off // tl
    full = lambda shape: pl.BlockSpec(shape, lambda i: (0,) * len(shape))
    return pl.pallas_call(
        functools.partial(_qkv_kernel, q_lora=q_lora, kv_lora=kv_lora,
                          q_scale=LOG2E / math.sqrt(QK_HEAD)),
        grid=(rows // tl,),
        in_specs=[pl.BlockSpec((tl, lat.shape[1]), lambda i: (off + i, 0)),
                  full((1, q_lora)), full((1, kv_lora)),
                  full((nh, q_lora, QK_PAD)), full((nh, kv_lora, QK_NOPE + V_HEAD)),
                  full((1, QK_PAD)), full((1, QK_PAD)),
                  pl.BlockSpec((tl, LANES), lambda i: (i, 0))],
        out_specs=[pl.BlockSpec((nh, tl, QK_PAD), lambda i: (0, i, 0)),
                   pl.BlockSpec((nh, tl, QK_PAD), lambda i: (0, i, 0)),
                   pl.BlockSpec((nh, tl, V_HEAD), lambda i: (0, i, 0))],
        out_shape=[jax.ShapeDtypeStruct((nh, rows, QK_PAD), BF16),
                   jax.ShapeDtypeStruct((nh, rows, QK_PAD), BF16),
                   jax.ShapeDtypeStruct((nh, rows, V_HEAD), BF16)],
        compiler_params=_cparams(("parallel",)),
        name="mla_qkv_proj",
    )(lat, q_lat_w, kv_lat_w, wq, wkv, gq, gk, tab)


def _attn_kernel(q_ref, k_ref, v_ref, o_ref, *, n_valid):
    q = q_ref[0]
    lp = k_ref.shape[1]
    kb = -(-n_valid // (ATTN_KEY_BLOCKS * QK_PAD)) * QK_PAD
    edges = list(range(0, min(lp, n_valid), kb)) + [min(lp, -(-n_valid // LANES) * LANES)]
    m = denom = acc = None
    for c0, c1 in zip(edges[:-1], edges[1:]):
        s = lax.dot_general(q, k_ref[0, c0:c1, :], (((1,), (1,)), ((), ())), preferred_element_type=F32)
        if c1 > n_valid:
            col = lax.broadcasted_iota(jnp.int32, s.shape, 1) + c0
            s = jnp.where(col < n_valid, s, NEG_BIG)
        mb = jnp.max(s, axis=-1, keepdims=True)
        if m is None:
            m = mb
            p = jnp.exp2(s - m)
            denom = jnp.sum(p, axis=-1, keepdims=True)
            acc = jnp.dot(p.astype(BF16), v_ref[0, c0:c1, :], preferred_element_type=F32)
        else:
            m_new = jnp.maximum(m, mb)
            alpha = jnp.exp2(m - m_new)
            p = jnp.exp2(s - m_new)
            denom = alpha * denom + jnp.sum(p, axis=-1, keepdims=True)
            acc = alpha * acc + jnp.dot(p.astype(BF16), v_ref[0, c0:c1, :], preferred_element_type=F32)
            m = m_new
    o_ref[...] = (acc / denom).astype(o_ref.dtype)


def _attention(q, k, v, batch, lp, n_valid):
    nh = q.shape[0]
    nq = next(n for n in range(1, lp + 1) if lp % n == 0 and (lp // n) % 16 == 0 and lp // n <= ATTN_MAX_Q_ROWS)
    tq = lp // nq
    return pl.pallas_call(
        functools.partial(_attn_kernel, n_valid=n_valid),
        grid=(batch, nh, nq),
        in_specs=[pl.BlockSpec((1, tq, QK_PAD), lambda b, h, i: (h, b * nq + i, 0)),
                  pl.BlockSpec((1, lp, QK_PAD), lambda b, h, i: (h, b, 0)),
                  pl.BlockSpec((1, lp, V_HEAD), lambda b, h, i: (h, b, 0))],
        out_specs=pl.BlockSpec((tq, V_HEAD), lambda b, h, i: (b * nq + i, h)),
        out_shape=jax.ShapeDtypeStruct((batch * lp, nh * V_HEAD), BF16),
        compiler_params=_cparams(("parallel", "parallel", "arbitrary")),
        name="mla_attention",
    )(q, k, v)


def _lower_bound(logit_ref, h, layer):
    lg = logit_ref[0, h]
    e = jnp.exp(lg - jnp.max(lg, axis=0, keepdims=True))
    p = e / jnp.sum(e, axis=0, keepdims=True)
    lb = jnp.zeros((1, lg.shape[1]), F32)
    for j in range(1, layer + 1):
        lb = lb + p[j:j + 1]
    return lb


def _gates(zq, zf, zi, lb, row_ok):
    q = zq * _sigmoid(zq)
    e = jnp.exp(-jnp.abs(zf))
    r = 1.0 / (1.0 + e)
    er = e * r
    pos = zf >= 0.0
    k = jnp.where(row_ok, (1.0 - lb) * jnp.where(pos, er, r), 0.0)
    f = lb + (1.0 - lb) * jnp.where(pos, r, er)
    g = jnp.maximum(jnp.log(f) * LOG2E, (jnp.log1p(-lb) * LOG2E - 1.0) + jnp.minimum(zf, 0.0) * LOG2E)
    return q, k, zi, g


def _log_cumsum(g, tri):
    g1 = g.astype(BF16)
    r1 = g - g1.astype(F32)
    g2 = r1.astype(BF16)
    g3 = (r1 - g2.astype(F32)).astype(BF16)
    bb = jnp.dot(tri, jnp.concatenate([g1, g2, g3], axis=1), preferred_element_type=F32)
    return (bb[:, :LANES] + bb[:, LANES:2 * LANES]) + bb[:, 2 * LANES:]


def _carried(q, k, v, b, total, st):
    qh = (q * jnp.exp2(b)).astype(BF16)
    kh = (k * jnp.exp2(total - b)).astype(BF16)
    o = lax.dot_general(qh, st.astype(BF16), (((1,), (1,)), ((), ())), preferred_element_type=F32)
    st_new = st * jnp.exp2(total) + lax.dot_general(
        v.astype(BF16), kh, (((0,), (0,)), ((), ())), preferred_element_type=F32)
    return o, st_new


def _hgrn_chunk_factorised(q, k, v, g, st, tri, in_scan, reverse):
    c_rows = q.shape[0]
    sub = HG_SUB_ROWS
    b = _log_cumsum(g, tri)
    total = b[0:1] if reverse else b[c_rows - 1:c_rows]
    o, st_new = _carried(q, k, v, b, total, st)

    zero_row = jnp.zeros((1, LANES), F32)
    growth = zero_row
    rows = []
    for blk in range(c_rows // sub):
        lo, hi = blk * sub, (blk + 1) * sub
        if reverse:
            ref = b[hi:hi + 1] if hi < c_rows else zero_row
            growth = jnp.maximum(growth, ref - b[lo:lo + 1])
        else:
            ref = b[lo - 1:lo] if lo > 0 else zero_row
            growth = jnp.maximum(growth, ref - b[hi - 1:hi])
        qi = (q[lo:hi] * jnp.exp2(b[lo:hi] - ref)).astype(BF16)
        ks = (k * jnp.exp2(jnp.minimum(ref - b, HG_MAX_LOG2_GROWTH))).astype(BF16)
        rows.append(lax.dot_general(qi, ks, (((1,), (1,)), ((), ())), preferred_element_type=F32))
    a = jnp.where(in_scan, jnp.concatenate(rows, axis=0), 0.0)
    o = o + jnp.dot(a.astype(BF16), v.astype(BF16), preferred_element_type=F32)
    return o, st_new, growth


def _hgrn_chunk_exact(q, k, v, g, st, tri, ones_sq, reverse):
    c_rows = q.shape[0]
    sub = HG_EXACT_SUB_ROWS
    nb = c_rows // sub
    b = _log_cumsum(g, tri)
    total = b[0:1] if reverse else b[c_rows - 1:c_rows]
    o, _ = _carried(q, k, v, b, total, st)

    vb = v.astype(BF16)
    pieces = []
    for blk in range(nb):
        lo, hi = blk * sub, (blk + 1) * sub
        if reverse:
            if blk == nb - 1:
                pieces.append(jnp.zeros((sub, LANES), F32))
                continue
            ref = b[hi:hi + 1]
            other = slice(hi, c_rows)
        else:
            if blk == 0:
                pieces.append(jnp.zeros((sub, LANES), F32))
                continue
            ref = b[lo - 1:lo]
            other = slice(0, lo)
        qi = (q[lo:hi] * jnp.exp2(b[lo:hi] - ref)).astype(BF16)
        ko = (k[other] * jnp.exp2(ref - b[other])).astype(BF16)
        a = lax.dot_general(qi, ko, (((1,), (1,)), ((), ())), preferred_element_type=F32)
        pieces.append(jnp.dot(a.astype(BF16), vb[other], preferred_element_type=F32))
    o = o + jnp.concatenate(pieces, axis=0)

    pos = lax.broadcasted_iota(jnp.int32, (c_rows, LANES), 0) % sub
    d_list, v_list = [], []
    for d in range(sub):
        if d == 0:
            vs = v
            dmat = q * k
        else:
            shift = (c_rows - d) if reverse else d
            ks = pltpu.roll(k, shift, axis=0)
            bs = pltpu.roll(b, shift, axis=0)
            vs = pltpu.roll(v, shift, axis=0)
            ok = (pos + d < sub) if reverse else (pos >= d)
            dmat = (q * ks) * jnp.exp2(jnp.where(ok, b - bs, NEG_BIG))
        d_list.append(dmat.astype(BF16))
        v_list.append(vs)
    rs = jnp.dot(jnp.concatenate(d_list, axis=0), ones_sq, preferred_element_type=F32)
    for d in range(sub):
        o = o + rs[d * c_rows:(d + 1) * c_rows] * v_list[d]
    return o


def _hgrn_kernel(info_ref, *refs, layer, reverse):
    if reverse:
        zq_ref, zf_ref, zi_ref, zg_ref, of_ref, lg_ref, nw_ref, o_ref, st_ref = refs
    else:
        zq_ref, zf_ref, zi_ref, lg_ref, o_ref, st_ref = refs
    step = pl.program_id(0)
    g = (pl.num_programs(0) - 1 - step) if reverse else step
    local, n_chunks, seq_rows = info_ref[3 * g], info_ref[3 * g + 1], info_ref[3 * g + 2]
    nh, c_rows = zq_ref.shape[0], zq_ref.shape[1]
    cur = step % 2

    @pl.when((local == n_chunks - 1) if reverse else (local == 0))
    def _():
        st_ref[cur] = jnp.zeros(st_ref.shape[1:], F32)

    rr = lax.broadcasted_iota(jnp.int32, (c_rows, c_rows), 0)
    cc = lax.broadcasted_iota(jnp.int32, (c_rows, c_rows), 1)
    in_scan = (cc >= rr) if reverse else (cc <= rr)
    tri = jnp.where(in_scan, 1.0, 0.0).astype(BF16)
    row_ok = (lax.broadcasted_iota(jnp.int32, (c_rows, LANES), 0) + local * c_rows) < seq_rows

    def finish(h, o):
        if reverse:
            o = o + of_ref[h]
            ms = jnp.mean(o * o, axis=-1, keepdims=True)
            zg = zg_ref[h]
            o_ref[h] = (((o * lax.rsqrt(ms + RMS_EPS)) * nw_ref[...]) * (zg * _sigmoid(zg))).astype(o_ref.dtype)
        else:
            o_ref[h] = o

    def fast_body(h, growth):
        q, k, v, gl = _gates(zq_ref[h], zf_ref[h], zi_ref[h], _lower_bound(lg_ref, h, layer), row_ok)
        o, st_new, gr = _hgrn_chunk_factorised(q, k, v, gl, st_ref[cur, h], tri, in_scan, reverse)
        st_ref[1 - cur, h] = st_new
        finish(h, o)
        return jnp.maximum(growth, gr)

    growth = lax.fori_loop(0, nh, fast_body, jnp.zeros((1, LANES), F32), unroll=16)

    @pl.when(jnp.max(growth) > HG_MAX_LOG2_GROWTH)
    def _():
        ones_sq = jnp.ones((LANES, LANES), BF16)

        def exact_body(h, carry):
            q, k, v, gl = _gates(zq_ref[h], zf_ref[h], zi_ref[h], _lower_bound(lg_ref, h, layer), row_ok)
            finish(h, _hgrn_chunk_exact(q, k, v, gl, st_ref[cur, h], tri, ones_sq, reverse))
            return carry

        lax.fori_loop(0, nh, exact_body, 0)


def _hgrn(hg, info, lb_logits, out_norm, layer, nh):
    m = hg.shape[1]
    c = HG_CHUNK_ROWS
    n_chunks = m // c
    depth = lb_logits.shape[1]
    lg = lb_logits.astype(F32).reshape(2, depth, nh, LANES).transpose(0, 2, 1, 3)

    def blk(grp, rev):
        if rev:
            return pl.BlockSpec((nh, c, LANES), lambda g, info_ref: (grp, n_chunks - 1 - g, 0))
        return pl.BlockSpec((nh, c, LANES), lambda g, info_ref: (grp, g, 0))

    state = pltpu.VMEM((2, nh, LANES, LANES), F32)
    o_f = pl.pallas_call(
        functools.partial(_hgrn_kernel, layer=layer, reverse=False),
        grid_spec=pltpu.PrefetchScalarGridSpec(
            num_scalar_prefetch=1, grid=(n_chunks,),
            in_specs=[blk(0, False), blk(1, False), blk(3, False),
                      pl.BlockSpec((1, nh, depth, LANES), lambda g, info_ref: (0, 0, 0, 0))],
            out_specs=blk(0, False),
            scratch_shapes=[state]),
        out_shape=jax.ShapeDtypeStruct((nh, m, LANES), F32),
        compiler_params=_cparams(("arbitrary",)),
        name="hgrn_forward",
    )(info, hg, hg, hg, lg)

    return pl.pallas_call(
        functools.partial(_hgrn_kernel, layer=layer, reverse=True),
        grid_spec=pltpu.PrefetchScalarGridSpec(
            num_scalar_prefetch=1, grid=(n_chunks,),
            in_specs=[blk(0, True), blk(2, True), blk(3, True), blk(4, True), blk(0, True),
                      pl.BlockSpec((1, nh, depth, LANES), lambda g, info_ref: (1, 0, 0, 0)),
                      pl.BlockSpec((1, LANES), lambda g, info_ref: (0, 0))],
            out_specs=blk(0, True),
            scratch_shapes=[state]),
        out_shape=jax.ShapeDtypeStruct((nh, m, LANES), BF16),
        compiler_params=_cparams(("arbitrary",)),
        name="hgrn_backward",
    )(info, hg, hg, hg, hg, o_f, lg, out_norm.reshape(1, LANES).astype(F32))


def _swap_halves(x, axis):
    a, b = jnp.split(x, 2, axis=axis)
    return jnp.concatenate([b, a], axis=axis)


def _rope_table(rows):
    inv_freq = 1.0 / (ROPE_THETA ** (jnp.arange(0, QK_ROPE, 2, dtype=F32) / QK_ROPE))
    ang = jnp.arange(rows, dtype=F32)[:, None] * inv_freq[None, :]
    c, s = jnp.cos(ang), jnp.sin(ang)
    return jnp.concatenate([c, c, -s, s], axis=1)


def _head_gain(g):
    g = g.astype(F32)
    return jnp.concatenate([g[:QK_NOPE], g[QK_NOPE:], _swap_halves(g[QK_NOPE:], 0)]).reshape(1, QK_PAD)


def _ffn_weights(norm_w, w_in, w_out):
    d_ff = w_out.shape[0]
    g = norm_w.astype(F32)[:, None]
    return (w_in[:, :d_ff] * g).astype(BF16), (w_in[:, d_ff:] * g).astype(BF16), w_out.astype(BF16)


def _ffn(h, hb, r, weights, scale=0.5):
    wg, wu, wo = weights
    return _matmul_residual(_ffn_in(hb, r, wg, wu), wo, h, scale)


def kernel(x_prompt, x_sample, meta_tokens, hgrn_lb_logits, norm_ffn1, w_ffn1_in, w_ffn1_out, norm_mix, w_in,
           q_lat_norm, w_uq, kv_lat_norm, w_ukv, q_head_norm, k_head_norm, hg_out_norm,
           w_branch_hgrn, w_branch_mla, w_out, norm_ffn2, w_ffn2_in, w_ffn2_out):
    depth = norm_ffn1.shape[0]
    d_model = x_prompt.shape[-1]
    hg_dim = hgrn_lb_logits.shape[-1]
    hg_heads = hg_dim // HG_HEAD_DIM
    q_lora = q_lat_norm.shape[-1]
    kv_lora = kv_lat_norm.shape[-1]
    mla_heads = w_uq.shape[-1] // QK_HEAD

    trunks = []
    row_off = 0
    blocks, info = [], []
    meta = meta_tokens.astype(F32)
    pad = jnp.zeros((SEQ_PAD - N_META, d_model), F32)
    for x in (x_prompt, x_sample):
        bsz, s, _ = x.shape
        lp = s + SEQ_PAD
        for b in range(bsz):
            blocks += [meta, x[b].astype(F32), pad]
        nc = lp // HG_CHUNK_ROWS
        for _ in range(bsz):
            for c in range(nc):
                info += [c, nc, s + N_META]
        trunks.append((bsz, s, lp, row_off))
        row_off += bsz * lp
    h = jnp.concatenate(blocks, axis=0)
    hb, r = _row_scale(h)
    info = jnp.asarray(info, jnp.int32)
    tabs = [jnp.tile(_rope_table(lp), (bsz, 1)) for (bsz, s, lp, _) in trunks]

    sizes = (hg_dim,) * 5 + (q_lora, kv_lora, QK_ROPE, d_model, d_model)
    offs = [0]
    for sz in sizes:
        offs.append(offs[-1] + sz)

    for l in range(depth):
        ffn1 = _ffn_weights(norm_ffn1[l], w_ffn1_in[l], w_ffn1_out[l])
        ffn2 = _ffn_weights(norm_ffn2[l], w_ffn2_in[l], w_ffn2_out[l])
        wl = w_in[l] * norm_mix[l].astype(F32)[:, None]
        w_hg = wl[:, :offs[5]].astype(BF16)
        w_kr = wl[:, offs[7]:offs[8]]
        w_lat = jnp.concatenate([wl[:, offs[5]:offs[7]], w_kr, _swap_halves(w_kr, 1)], axis=1).astype(BF16)
        w_gh = wl[:, offs[8]:offs[9]].astype(BF16)
        w_ga = wl[:, offs[9]:offs[10]].astype(BF16)
        wq = w_uq[l].reshape(q_lora, mla_heads, QK_HEAD)
        wq = jnp.concatenate([wq, _swap_halves(wq[:, :, QK_NOPE:], 2)], axis=2)
        wq = wq.transpose(1, 0, 2).astype(BF16)
        wkv = w_ukv[l].reshape(kv_lora, mla_heads, QK_NOPE + V_HEAD).transpose(1, 0, 2).astype(BF16)

        h, hb, r = _ffn(h, hb, r, ffn1)

        hg = _matmul(hb, r, w_hg, head_major=True)
        lat = _matmul(hb, r, w_lat)
        o_h = _hgrn(hg, info, hgrn_lb_logits, hg_out_norm[l], l, hg_heads)

        o_a = []
        for (bsz, s, lp, off), tab in zip(trunks, tabs):
            q, k, v = _qkv_proj(lat, q_lat_norm[l].reshape(1, q_lora).astype(F32),
                                kv_lat_norm[l].reshape(1, kv_lora).astype(F32), wq, wkv,
                                _head_gain(q_head_norm[l]), _head_gain(k_head_norm[l]),
                                tab, off, bsz * lp, q_lora, kv_lora)
            o_a.append(_attention(q, k, v, bsz, lp, s + N_META))
        o_a = jnp.concatenate(o_a, axis=0)

        merged = _merge(hb, r, o_h, o_a, w_gh, w_ga, w_branch_hgrn[l].astype(BF16), w_branch_mla[l].astype(BF16))
        h, hb, r = _matmul_residual(merged, w_out[l].astype(BF16), h, 1.0)

        h, hb, r = _ffn(h, hb, r, ffn2)

    outs = []
    for (bsz, s, lp, off) in trunks:
        outs.append(jnp.stack([h[off + b * lp + N_META:off + b * lp + N_META + s] for b in range(bsz)]))
    return tuple(outs)
```

```python
import functools
import math

import jax
import jax.numpy as jnp
from jax import lax
from jax.experimental import pallas as pl
from jax.experimental.pallas import tpu as pltpu

N_META = 16
HG_HEAD_DIM = 128
QK_NOPE = 128
QK_ROPE = 64
QK_HEAD = QK_NOPE + QK_ROPE
V_HEAD = 128
ROPE_THETA = 10000.0
RMS_EPS = 1e-6

LANES = 128
SEQ_PAD = 128
HG_CHUNK_ROWS = 128
ROW_TILES = (1024, 512, 256, 128, 64, 32, 16)
HG_SUB_ROWS = 32
HG_EXACT_SUB_ROWS = 16
HG_MAX_LOG2_GROWTH = 115.0
ATTN_MAX_Q_ROWS = 640
ATTN_KEY_BLOCKS = 3
QK_PAD = 256
VMEM_LIMIT = 56 * 1024 * 1024
VMEM_LIMIT_LARGE = 58 * 1024 * 1024
F32 = jnp.float32
BF16 = jnp.bfloat16
NEG_BIG = -1e30
LN2 = math.log(2.0)
LOG2E = 1.0 / LN2


def _cparams(sem, vmem_limit=VMEM_LIMIT):
    return pltpu.CompilerParams(dimension_semantics=sem, vmem_limit_bytes=vmem_limit)


def _pick(n, prefs):
    for p in prefs:
        if n % p == 0:
            return p
    return n


def _sigmoid(x):
    return 1.0 / (1.0 + jnp.exp(-x))


def _row_scale_kernel(x_ref, hb_ref, r_ref):
    x = x_ref[...]
    hb_ref[...] = x.astype(BF16)
    r_ref[...] = lax.rsqrt(jnp.mean(x * x, axis=-1, keepdims=True) + RMS_EPS)


def _row_scale(x):
    m, d = x.shape
    tm = _pick(m, (512, 256, 128, 64, 32, 16, 8))
    return pl.pallas_call(
        _row_scale_kernel,
        grid=(m // tm,),
        in_specs=[pl.BlockSpec((tm, d), lambda i: (i, 0))],
        out_specs=[pl.BlockSpec((tm, d), lambda i: (i, 0)),
                   pl.BlockSpec((tm, 1), lambda i: (i, 0))],
        out_shape=[jax.ShapeDtypeStruct((m, d), BF16), jax.ShapeDtypeStruct((m, 1), F32)],
        compiler_params=_cparams(("parallel",)),
        name="row_scale",
    )(x)


def _mm_kernel(a_ref, r_ref, b_ref, o_ref):
    o_ref[...] = jnp.dot(a_ref[...], b_ref[...], preferred_element_type=F32) * r_ref[...]


def _mm_headmajor_kernel(a_ref, r_ref, b_ref, o_ref):
    acc = jnp.dot(a_ref[...], b_ref[...], preferred_element_type=F32) * r_ref[...]
    for j in range(o_ref.shape[0]):
        o_ref[j] = acc[:, j * LANES:(j + 1) * LANES]


def _matmul(a, r, b, *, head_major=False):
    m, k = a.shape
    n = b.shape[1]
    if n % 256 == 0:
        tm = _pick(m, ROW_TILES)
        tn = _pick(n, (512, 256))
    else:
        tm = _pick(m, (512, 256, 128, 64, 32, 16))
        tn = n
    grid = (m // tm, n // tn)
    in_specs = [pl.BlockSpec((tm, k), lambda i, j: (i, 0)),
                pl.BlockSpec((tm, 1), lambda i, j: (i, 0)),
                pl.BlockSpec((k, tn), lambda i, j: (0, j))]
    if head_major:
        return pl.pallas_call(
            _mm_headmajor_kernel, grid=grid, in_specs=in_specs,
            out_specs=pl.BlockSpec((tn // LANES, tm, LANES), lambda i, j: (j, i, 0)),
            out_shape=jax.ShapeDtypeStruct((n // LANES, m, LANES), F32),
            compiler_params=_cparams(("parallel", "arbitrary")),
            name="proj_headmajor",
        )(a, r, b)
    return pl.pallas_call(
        _mm_kernel, grid=grid, in_specs=in_specs,
        out_specs=pl.BlockSpec((tm, tn), lambda i, j: (i, j)),
        out_shape=jax.ShapeDtypeStruct((m, n), F32),
        compiler_params=_cparams(("parallel", "arbitrary")),
        name="proj",
    )(a, r, b)


def _ffn_in_kernel(x_ref, r_ref, wg_ref, wu_ref, o_ref):
    x = x_ref[...]
    r = r_ref[...]
    g = jnp.dot(x, wg_ref[...], preferred_element_type=F32) * r
    u = jnp.dot(x, wu_ref[...], preferred_element_type=F32) * r
    o_ref[...] = ((g * _sigmoid(g)) * u).astype(o_ref.dtype)


def _ffn_in(x, r, wg, wu):
    m, k = x.shape
    f = wg.shape[1]
    tm = _pick(m, ROW_TILES)
    tf = min(512, f)
    return pl.pallas_call(
        _ffn_in_kernel,
        grid=(m // tm, pl.cdiv(f, tf)),
        in_specs=[pl.BlockSpec((tm, k), lambda i, j: (i, 0)),
                  pl.BlockSpec((tm, 1), lambda i, j: (i, 0)),
                  pl.BlockSpec((k, tf), lambda i, j: (0, j)),
                  pl.BlockSpec((k, tf), lambda i, j: (0, j))],
        out_specs=pl.BlockSpec((tm, tf), lambda i, j: (i, j)),
        out_shape=jax.ShapeDtypeStruct((m, f), BF16),
        compiler_params=_cparams(("parallel", "arbitrary")),
        name="ffn_in",
    )(x, r, wg, wu)


def _mm_res_kernel(a_ref, b_ref, res_ref, o_ref, hb_ref, rs_ref, ss_ref, *, scale, k_tail):
    j, kk = pl.program_id(1), pl.program_id(2)

    @pl.when(kk == 0)
    def _():
        o_ref[...] = res_ref[...]

    last = pl.num_programs(2) - 1

    @pl.when(kk < last)
    def _():
        o_ref[...] += scale * jnp.dot(a_ref[...], b_ref[...], preferred_element_type=F32)

    @pl.when(kk == last)
    def _():
        a, b = a_ref[...], b_ref[...]
        if k_tail:
            a = jnp.where(lax.broadcasted_iota(jnp.int32, a.shape, 1) < k_tail, a, jnp.zeros_like(a))
            b = jnp.where(lax.broadcasted_iota(jnp.int32, b.shape, 0) < k_tail, b, jnp.zeros_like(b))
        o = o_ref[...] + scale * jnp.dot(a, b, preferred_element_type=F32)
        o_ref[...] = o
        hb_ref[...] = o.astype(BF16)
        part = jnp.sum(o * o, axis=-1, keepdims=True)
        ss = jnp.where(j == 0, part, ss_ref[...] + part)
        ss_ref[...] = ss
        d_model = o.shape[1] * pl.num_programs(1)
        rs_ref[...] = lax.rsqrt(ss * (1.0 / d_model) + RMS_EPS)


def _matmul_residual(a, b, res, scale):
    m, k = a.shape
    n = b.shape[1]
    tm = _pick(m, ROW_TILES)
    tn = _pick(n, (2048, 1024, 512, 256, 128))
    tk = min(1024, k)
    return pl.pallas_call(
        functools.partial(_mm_res_kernel, scale=scale, k_tail=k % tk),
        grid=(m // tm, n // tn, pl.cdiv(k, tk)),
        in_specs=[pl.BlockSpec((tm, tk), lambda i, j, kk: (i, kk)),
                  pl.BlockSpec((tk, tn), lambda i, j, kk: (kk, j)),
                  pl.BlockSpec((tm, tn), lambda i, j, kk: (i, j))],
        out_specs=[pl.BlockSpec((tm, tn), lambda i, j, kk: (i, j)),
                   pl.BlockSpec((tm, tn), lambda i, j, kk: (i, j)),
                   pl.BlockSpec((tm, 1), lambda i, j, kk: (i, 0))],
        out_shape=[jax.ShapeDtypeStruct((m, n), F32), jax.ShapeDtypeStruct((m, n), BF16),
                   jax.ShapeDtypeStruct((m, 1), F32)],
        scratch_shapes=[pltpu.VMEM((tm, 1), F32)],
        compiler_params=_cparams(("parallel", "arbitrary", "arbitrary"), VMEM_LIMIT_LARGE),
        name="proj_residual",
    )(a, b, res)


def _merge_kernel(u_ref, r_ref, oh_ref, oa_ref, wgh_ref, wga_ref, wbh_ref, wba_ref, o_ref):
    u = u_ref[...]
    r = r_ref[...]
    oh = jnp.concatenate([oh_ref[h] for h in range(oh_ref.shape[0])], axis=1)
    gh = jnp.dot(u, wgh_ref[...], preferred_element_type=F32) * r
    ga = jnp.dot(u, wga_ref[...], preferred_element_type=F32) * r
    bh = jnp.dot(oh, wbh_ref[...], preferred_element_type=F32)
    ba = jnp.dot(oa_ref[...], wba_ref[...], preferred_element_type=F32)
    o_ref[...] = (_sigmoid(gh) * bh + _sigmoid(ga) * ba).astype(o_ref.dtype)


def _merge(u, r, oh, oa, wgh, wga, wbh, wba):
    m, d = u.shape
    nh = oh.shape[0]
    n = wgh.shape[1]
    tm = _pick(m, ROW_TILES)
    tn = _pick(n, (256, 128))
    return pl.pallas_call(
        _merge_kernel,
        grid=(m // tm, n // tn),
        in_specs=[pl.BlockSpec((tm, d), lambda i, j: (i, 0)),
                  pl.BlockSpec((tm, 1), lambda i, j: (i, 0)),
                  pl.BlockSpec((nh, tm, LANES), lambda i, j: (0, i, 0)),
                  pl.BlockSpec((tm, oa.shape[1]), lambda i, j: (i, 0)),
                  pl.BlockSpec((d, tn), lambda i, j: (0, j)),
                  pl.BlockSpec((d, tn), lambda i, j: (0, j)),
                  pl.BlockSpec((wbh.shape[0], tn), lambda i, j: (0, j)),
                  pl.BlockSpec((wba.shape[0], tn), lambda i, j: (0, j))],
        out_specs=pl.BlockSpec((tm, tn), lambda i, j: (i, j)),
        out_shape=jax.ShapeDtypeStruct((m, n), BF16),
        compiler_params=_cparams(("parallel", "arbitrary")),
        name="branch_merge",
    )(u, r, oh, oa, wgh, wga, wbh, wba)


def _rope_tail(y2, tab):
    w = y2 * tab
    rot = w + pltpu.roll(w, QK_ROPE, axis=1)
    lane = lax.broadcasted_iota(jnp.int32, rot.shape, 1)
    return jnp.where(lane < QK_ROPE, rot, 0.0)


def _rope_sumsq(y2):
    lane = lax.broadcasted_iota(jnp.int32, y2.shape, 1)
    y2m = jnp.where(lane < QK_ROPE, y2, 0.0)
    return jnp.sum(y2m * y2m, axis=-1, keepdims=True)


def _head_rms_scale(y, ones_mask, extra_ss):
    ss = jnp.dot((y * y).astype(BF16), ones_mask, preferred_element_type=F32)
    return lax.rsqrt((ss + extra_ss) * (1.0 / QK_HEAD) + RMS_EPS)


def _latent_norm(x, w):
    ms = jnp.mean(x * x, axis=-1, keepdims=True)
    return ((x * lax.rsqrt(ms + RMS_EPS)) * w).astype(BF16)


def _qkv_kernel(lat_ref, qlw_ref, kvlw_ref, wq_ref, wkv_ref, gq_ref, gk_ref, tab_ref,
                q_ref, k_ref, v_ref, *, q_lora, kv_lora, q_scale):
    xq = _latent_norm(lat_ref[:, 0:q_lora], qlw_ref[...])
    xkv = _latent_norm(lat_ref[:, q_lora:q_lora + kv_lora], kvlw_ref[...])
    kr = lat_ref[:, q_lora + kv_lora:q_lora + kv_lora + LANES]
    gq1 = gq_ref[:, :QK_NOPE] * q_scale
    gk1 = gk_ref[:, :QK_NOPE]
    tab_q = tab_ref[...] * (gq_ref[:, QK_NOPE:] * q_scale)
    tab_k = tab_ref[...] * gk_ref[:, QK_NOPE:]
    kr_ss = _rope_sumsq(kr)
    row = lax.broadcasted_iota(jnp.int32, (QK_PAD, LANES), 0)
    ones_q = jnp.where(row < QK_HEAD, 1.0, 0.0).astype(BF16)
    ones_k = jnp.ones((QK_NOPE, LANES), BF16)

    def body(h, carry):
        y = jnp.dot(xq, wq_ref[h], preferred_element_type=F32)
        y1, y2 = y[:, :QK_NOPE], y[:, QK_NOPE:]
        r = _head_rms_scale(y, ones_q, 0.0)
        q_ref[h, :, 0:QK_NOPE] = ((y1 * r) * gq1).astype(BF16)
        q_ref[h, :, QK_NOPE:] = _rope_tail(y2 * r, tab_q).astype(BF16)

        z = jnp.dot(xkv, wkv_ref[h], preferred_element_type=F32)
        z1 = z[:, :QK_NOPE]
        rk = _head_rms_scale(z1, ones_k, kr_ss)
        k_ref[h, :, 0:QK_NOPE] = ((z1 * rk) * gk1).astype(BF16)
        k_ref[h, :, QK_NOPE:] = _rope_tail(kr * rk, tab_k).astype(BF16)
        v_ref[h] = z[:, QK_NOPE:].astype(BF16)
        return carry

    lax.fori_loop(0, q_ref.shape[0], body, 0, unroll=2)


def _qkv_proj(lat, q_lat_w, kv_lat_w, wq, wkv, gq, gk, tab, row_off, rows, q_lora, kv_lora):
    nh = wq.shape[0]
    tl = _pick(math.gcd(rows, row_off) if row_off else rows, (512, 256, 128, 64, 32, 16))
    off = row_off // tl
    full = lambda shape: pl.BlockSpec(shape, lambda i: (0,) * len(shape))
    return pl.pallas_call(
        functools.partial(_qkv_kernel, q_lora=q_lora, kv_lora=kv_lora,
                          q_scale=LOG2E / math.sqrt(QK_HEAD)),
        grid=(rows // tl,),
        in_specs=[pl.BlockSpec((tl, lat.shape[1]), lambda i: (off + i, 0)),
                  full((1, q_lora)), full((1, kv_lora)),
                  full((nh, q_lora, QK_PAD)), full((nh, kv_lora, QK_NOPE + V_HEAD)),
                  full((1, QK_PAD)), full((1, QK_PAD)),
                  pl.BlockSpec((tl, LANES), lambda i: (i, 0))],
        out_specs=[pl.BlockSpec((nh, tl, QK_PAD), lambda i: (0, i, 0)),
                   pl.BlockSpec((nh, tl, QK_PAD), lambda i: (0, i, 0)),
                   pl.BlockSpec((nh, tl, V_HEAD), lambda i: (0, i, 0))],
        out_shape=[jax.ShapeDtypeStruct((nh, rows, QK_PAD), BF16),
                   jax.ShapeDtypeStruct((nh, rows, QK_PAD), BF16),
                   jax.ShapeDtypeStruct((nh, rows, V_HEAD), BF16)],
        compiler_params=_cparams(("parallel",)),
        name="mla_qkv_proj",
    )(lat, q_lat_w, kv_lat_w, wq, wkv, gq, gk, tab)


def _attn_kernel(q_ref, k_ref, v_ref, o_ref, *, n_valid):
    q = q_ref[0]
    lp = k_ref.shape[1]
    kb = -(-n_valid // (ATTN_KEY_BLOCKS * QK_PAD)) * QK_PAD
    edges = list(range(0, min(lp, n_valid), kb)) + [min(lp, -(-n_valid // LANES) * LANES)]
    m = denom = acc = None
    for c0, c1 in zip(edges[:-1], edges[1:]):
        s = lax.dot_general(q, k_ref[0, c0:c1, :], (((1,), (1,)), ((), ())), preferred_element_type=F32)
        if c1 > n_valid:
            col = lax.broadcasted_iota(jnp.int32, s.shape, 1) + c0
            s = jnp.where(col < n_valid, s, NEG_BIG)
        mb = jnp.max(s, axis=-1, keepdims=True)
        if m is None:
            m = mb
            p = jnp.exp2(s - m)
            denom = jnp.sum(p, axis=-1, keepdims=True)
            acc = jnp.dot(p.astype(BF16), v_ref[0, c0:c1, :], preferred_element_type=F32)
        else:
            m_new = jnp.maximum(m, mb)
            alpha = jnp.exp2(m - m_new)
            p = jnp.exp2(s - m_new)
            denom = alpha * denom + jnp.sum(p, axis=-1, keepdims=True)
            acc = alpha * acc + jnp.dot(p.astype(BF16), v_ref[0, c0:c1, :], preferred_element_type=F32)
            m = m_new
    o_ref[...] = (acc / denom).astype(o_ref.dtype)


def _attention(q, k, v, batch, lp, n_valid):
    nh = q.shape[0]
    nq = next(n for n in range(1, lp + 1) if lp % n == 0 and (lp // n) % 16 == 0 and lp // n <= ATTN_MAX_Q_ROWS)
    tq = lp // nq
    return pl.pallas_call(
        functools.partial(_attn_kernel, n_valid=n_valid),
        grid=(batch, nh, nq),
        in_specs=[pl.BlockSpec((1, tq, QK_PAD), lambda b, h, i: (h, b * nq + i, 0)),
                  pl.BlockSpec((1, lp, QK_PAD), lambda b, h, i: (h, b, 0)),
                  pl.BlockSpec((1, lp, V_HEAD), lambda b, h, i: (h, b, 0))],
        out_specs=pl.BlockSpec((tq, V_HEAD), lambda b, h, i: (b * nq + i, h)),
        out_shape=jax.ShapeDtypeStruct((batch * lp, nh * V_HEAD), BF16),
        compiler_params=_cparams(("parallel", "parallel", "arbitrary")),
        name="mla_attention",
    )(q, k, v)


def _lower_bound(logit_ref, h, layer):
    lg = logit_ref[0, h]
    e = jnp.exp(lg - jnp.max(lg, axis=0, keepdims=True))
    p = e / jnp.sum(e, axis=0, keepdims=True)
    lb = jnp.zeros((1, lg.shape[1]), F32)
    for j in range(1, layer + 1):
        lb = lb + p[j:j + 1]
    return lb


def _gates(zq, zf, zi, lb, row_ok):
    q = zq * _sigmoid(zq)
    e = jnp.exp(-jnp.abs(zf))
    r = 1.0 / (1.0 + e)
    er = e * r
    pos = zf >= 0.0
    k = jnp.where(row_ok, (1.0 - lb) * jnp.where(pos, er, r), 0.0)
    f = lb + (1.0 - lb) * jnp.where(pos, r, er)
    g = jnp.maximum(jnp.log(f) * LOG2E, (jnp.log1p(-lb) * LOG2E - 1.0) + jnp.minimum(zf, 0.0) * LOG2E)
    return q, k, zi, g


def _log_cumsum(g, tri):
    g1 = g.astype(BF16)
    r1 = g - g1.astype(F32)
    g2 = r1.astype(BF16)
    g3 = (r1 - g2.astype(F32)).astype(BF16)
    bb = jnp.dot(tri, jnp.concatenate([g1, g2, g3], axis=1), preferred_element_type=F32)
    return (bb[:, :LANES] + bb[:, LANES:2 * LANES]) + bb[:, 2 * LANES:]


def _carried(q, k, v, b, total, st):
    qh = (q * jnp.exp2(b)).astype(BF16)
    kh = (k * jnp.exp2(total - b)).astype(BF16)
    o = lax.dot_general(qh, st.astype(BF16), (((1,), (1,)), ((), ())), preferred_element_type=F32)
    st_new = st * jnp.exp2(total) + lax.dot_general(
        v.astype(BF16), kh, (((0,), (0,)), ((), ())), preferred_element_type=F32)
    return o, st_new


def _hgrn_chunk_factorised(q, k, v, g, st, tri, in_scan, reverse):
    c_rows = q.shape[0]
    sub = HG_SUB_ROWS
    b = _log_cumsum(g, tri)
    total = b[0:1] if reverse else b[c_rows - 1:c_rows]
    o, st_new = _carried(q, k, v, b, total, st)

    zero_row = jnp.zeros((1, LANES), F32)
    growth = zero_row
    rows = []
    for blk in range(c_rows // sub):
        lo, hi = blk * sub, (blk + 1) * sub
        if reverse:
            ref = b[hi:hi + 1] if hi < c_rows else zero_row
            growth = jnp.maximum(growth, ref - b[lo:lo + 1])
        else:
            ref = b[lo - 1:lo] if lo > 0 else zero_row
            growth = jnp.maximum(growth, ref - b[hi - 1:hi])
        qi = (q[lo:hi] * jnp.exp2(b[lo:hi] - ref)).astype(BF16)
        ks = (k * jnp.exp2(jnp.minimum(ref - b, HG_MAX_LOG2_GROWTH))).astype(BF16)
        rows.append(lax.dot_general(qi, ks, (((1,), (1,)), ((), ())), preferred_element_type=F32))
    a = jnp.where(in_scan, jnp.concatenate(rows, axis=0), 0.0)
    o = o + jnp.dot(a.astype(BF16), v.astype(BF16), preferred_element_type=F32)
    return o, st_new, growth


def _hgrn_chunk_exact(q, k, v, g, st, tri, ones_sq, reverse):
    c_rows = q.shape[0]
    sub = HG_EXACT_SUB_ROWS
    nb = c_rows // sub
    b = _log_cumsum(g, tri)
    total = b[0:1] if reverse else b[c_rows - 1:c_rows]
    o, _ = _carried(q, k, v, b, total, st)

    vb = v.astype(BF16)
    pieces = []
    for blk in range(nb):
        lo, hi = blk * sub, (blk + 1) * sub
        if reverse:
            if blk == nb - 1:
                pieces.append(jnp.zeros((sub, LANES), F32))
                continue
            ref = b[hi:hi + 1]
            other = slice(hi, c_rows)
        else:
            if blk == 0:
                pieces.append(jnp.zeros((sub, LANES), F32))
                continue
            ref = b[lo - 1:lo]
            other = slice(0, lo)
        qi = (q[lo:hi] * jnp.exp2(b[lo:hi] - ref)).astype(BF16)
        ko = (k[other] * jnp.exp2(ref - b[other])).astype(BF16)
        a = lax.dot_general(qi, ko, (((1,), (1,)), ((), ())), preferred_element_type=F32)
        pieces.append(jnp.dot(a.astype(BF16), vb[other], preferred_element_type=F32))
    o = o + jnp.concatenate(pieces, axis=0)

    pos = lax.broadcasted_iota(jnp.int32, (c_rows, LANES), 0) % sub
    d_list, v_list = [], []
    for d in range(sub):
        if d == 0:
            vs = v
            dmat = q * k
        else:
            shift = (c_rows - d) if reverse else d
            ks = pltpu.roll(k, shift, axis=0)
            bs = pltpu.roll(b, shift, axis=0)
            vs = pltpu.roll(v, shift, axis=0)
            ok = (pos + d < sub) if reverse else (pos >= d)
            dmat = (q * ks) * jnp.exp2(jnp.where(ok, b - bs, NEG_BIG))
        d_list.append(dmat.astype(BF16))
        v_list.append(vs)
    rs = jnp.dot(jnp.concatenate(d_list, axis=0), ones_sq, preferred_element_type=F32)
    for d in range(sub):
        o = o + rs[d * c_rows:(d + 1) * c_rows] * v_list[d]
    return o


def _hgrn_kernel(info_ref, *refs, layer, reverse):
    if reverse:
        zq_ref, zf_ref, zi_ref, zg_ref, of_ref, lg_ref, nw_ref, o_ref, st_ref = refs
    else:
        zq_ref, zf_ref, zi_ref, lg_ref, o_ref, st_ref = refs
    step = pl.program_id(0)
    g = (pl.num_programs(0) - 1 - step) if reverse else step
    local, n_chunks, seq_rows = info_ref[3 * g], info_ref[3 * g + 1], info_ref[3 * g + 2]
    nh, c_rows = zq_ref.shape[0], zq_ref.shape[1]
    cur = step % 2

    @pl.when((local == n_chunks - 1) if reverse else (local == 0))
    def _():
        st_ref[cur] = jnp.zeros(st_ref.shape[1:], F32)

    rr = lax.broadcasted_iota(jnp.int32, (c_rows, c_rows), 0)
    cc = lax.broadcasted_iota(jnp.int32, (c_rows, c_rows), 1)
    in_scan = (cc >= rr) if reverse else (cc <= rr)
    tri = jnp.where(in_scan, 1.0, 0.0).astype(BF16)
    row_ok = (lax.broadcasted_iota(jnp.int32, (c_rows, LANES), 0) + local * c_rows) < seq_rows

    def finish(h, o):
        if reverse:
            o = o + of_ref[h]
            ms = jnp.mean(o * o, axis=-1, keepdims=True)
            zg = zg_ref[h]
            o_ref[h] = (((o * lax.rsqrt(ms + RMS_EPS)) * nw_ref[...]) * (zg * _sigmoid(zg))).astype(o_ref.dtype)
        else:
            o_ref[h] = o

    def fast_body(h, growth):
        q, k, v, gl = _gates(zq_ref[h], zf_ref[h], zi_ref[h], _lower_bound(lg_ref, h, layer), row_ok)
        o, st_new, gr = _hgrn_chunk_factorised(q, k, v, gl, st_ref[cur, h], tri, in_scan, reverse)
        st_ref[1 - cur, h] = st_new
        finish(h, o)
        return jnp.maximum(growth, gr)

    growth = lax.fori_loop(0, nh, fast_body, jnp.zeros((1, LANES), F32), unroll=16)

    @pl.when(jnp.max(growth) > HG_MAX_LOG2_GROWTH)
    def _():
        ones_sq = jnp.ones((LANES, LANES), BF16)

        def exact_body(h, carry):
            q, k, v, gl = _gates(zq_ref[h], zf_ref[h], zi_ref[h], _lower_bound(lg_ref, h, layer), row_ok)
            finish(h, _hgrn_chunk_exact(q, k, v, gl, st_ref[cur, h], tri, ones_sq, reverse))
            return carry

        lax.fori_loop(0, nh, exact_body, 0)


def _hgrn(hg, info, lb_logits, out_norm, layer, nh):
    m = hg.shape[1]
    c = HG_CHUNK_ROWS
    n_chunks = m // c
    depth = lb_logits.shape[1]
    lg = lb_logits.astype(F32).reshape(2, depth, nh, LANES).transpose(0, 2, 1, 3)

    def blk(grp, rev):
        if rev:
            return pl.BlockSpec((nh, c, LANES), lambda g, info_ref: (grp, n_chunks - 1 - g, 0))
        return pl.BlockSpec((nh, c, LANES), lambda g, info_ref: (grp, g, 0))

    state = pltpu.VMEM((2, nh, LANES, LANES), F32)
    o_f = pl.pallas_call(
        functools.partial(_hgrn_kernel, layer=layer, reverse=False),
        grid_spec=pltpu.PrefetchScalarGridSpec(
            num_scalar_prefetch=1, grid=(n_chunks,),
            in_specs=[blk(0, False), blk(1, False), blk(3, False),
                      pl.BlockSpec((1, nh, depth, LANES), lambda g, info_ref: (0, 0, 0, 0))],
            out_specs=blk(0, False),
            scratch_shapes=[state]),
        out_shape=jax.ShapeDtypeStruct((nh, m, LANES), F32),
        compiler_params=_cparams(("arbitrary",)),
        name="hgrn_forward",
    )(info, hg, hg, hg, lg)

    return pl.pallas_call(
        functools.partial(_hgrn_kernel, layer=layer, reverse=True),
        grid_spec=pltpu.PrefetchScalarGridSpec(
            num_scalar_prefetch=1, grid=(n_chunks,),
            in_specs=[blk(0, True), blk(2, True), blk(3, True), blk(4, True), blk(0, True),
                      pl.BlockSpec((1, nh, depth, LANES), lambda g, info_ref: (1, 0, 0, 0)),
                      pl.BlockSpec((1, LANES), lambda g, info_ref: (0, 0))],
            out_specs=blk(0, True),
            scratch_shapes=[state]),
        out_shape=jax.ShapeDtypeStruct((nh, m, LANES), BF16),
        compiler_params=_cparams(("arbitrary",)),
        name="hgrn_backward",
    )(info, hg, hg, hg, hg, o_f, lg, out_norm.reshape(1, LANES).astype(F32))


def _swap_halves(x, axis):
    a, b = jnp.split(x, 2, axis=axis)
    return jnp.concatenate([b, a], axis=axis)


def _rope_table(rows):
    inv_freq = 1.0 / (ROPE_THETA ** (jnp.arange(0, QK_ROPE, 2, dtype=F32) / QK_ROPE))
    ang = jnp.arange(rows, dtype=F32)[:, None] * inv_freq[None, :]
    c, s = jnp.cos(ang), jnp.sin(ang)
    return jnp.concatenate([c, c, -s, s], axis=1)


def _head_gain(g):
    g = g.astype(F32)
    return jnp.concatenate([g[:QK_NOPE], g[QK_NOPE:], _swap_halves(g[QK_NOPE:], 0)]).reshape(1, QK_PAD)


def _ffn_weights(norm_w, w_in, w_out):
    d_ff = w_out.shape[0]
    g = norm_w.astype(F32)[:, None]
    return (w_in[:, :d_ff] * g).astype(BF16), (w_in[:, d_ff:] * g).astype(BF16), w_out.astype(BF16)


def _ffn(h, hb, r, weights, scale=0.5):
    wg, wu, wo = weights
    return _matmul_residual(_ffn_in(hb, r, wg, wu), wo, h, scale)


def kernel(x_prompt, x_sample, meta_tokens, hgrn_lb_logits, norm_ffn1, w_ffn1_in, w_ffn1_out, norm_mix, w_in,
           q_lat_norm, w_uq, kv_lat_norm, w_ukv, q_head_norm, k_head_norm, hg_out_norm,
           w_branch_hgrn, w_branch_mla, w_out, norm_ffn2, w_ffn2_in, w_ffn2_out):
    depth = norm_ffn1.shape[0]
    d_model = x_prompt.shape[-1]
    hg_dim = hgrn_lb_logits.shape[-1]
    hg_heads = hg_dim // HG_HEAD_DIM
    q_lora = q_lat_norm.shape[-1]
    kv_lora = kv_lat_norm.shape[-1]
    mla_heads = w_uq.shape[-1] // QK_HEAD

    trunks = []
    row_off = 0
    blocks, info = [], []
    meta = meta_tokens.astype(F32)
    pad = jnp.zeros((SEQ_PAD - N_META, d_model), F32)
    for x in (x_prompt, x_sample):
        bsz, s, _ = x.shape
        lp = s + SEQ_PAD
        for b in range(bsz):
            blocks += [meta, x[b].astype(F32), pad]
        nc = lp // HG_CHUNK_ROWS
        for _ in range(bsz):
            for c in range(nc):
                info += [c, nc, s + N_META]
        trunks.append((bsz, s, lp, row_off))
        row_off += bsz * lp
    h = jnp.concatenate(blocks, axis=0)
    hb, r = _row_scale(h)
    info = jnp.asarray(info, jnp.int32)
    tabs = [jnp.tile(_rope_table(lp), (bsz, 1)) for (bsz, s, lp, _) in trunks]

    sizes = (hg_dim,) * 5 + (q_lora, kv_lora, QK_ROPE, d_model, d_model)
    offs = [0]
    for sz in sizes:
        offs.append(offs[-1] + sz)

    for l in range(depth):
        ffn1 = _ffn_weights(norm_ffn1[l], w_ffn1_in[l], w_ffn1_out[l])
        ffn2 = _ffn_weights(norm_ffn2[l], w_ffn2_in[l], w_ffn2_out[l])
        wl = w_in[l]
        gm = norm_mix[l].astype(F32)[:, None]
        w_hg = (wl[:, :offs[5]] * gm).astype(BF16)
        w_kr = wl[:, offs[7]:offs[8]]
        w_lat = (jnp.concatenate([wl[:, offs[5]:offs[7]], w_kr, _swap_halves(w_kr, 1)], axis=1) * gm).astype(BF16)
        w_gh = (wl[:, offs[8]:offs[9]] * gm).astype(BF16)
        w_ga = (wl[:, offs[9]:offs[10]] * gm).astype(BF16)
        wq = w_uq[l].reshape(q_lora, mla_heads, QK_HEAD)
        wq = jnp.concatenate([wq, _swap_halves(wq[:, :, QK_NOPE:], 2)], axis=2)
        wq = wq.transpose(1, 0, 2).astype(BF16)
        wkv = w_ukv[l].reshape(kv_lora, mla_heads, QK_NOPE + V_HEAD).transpose(1, 0, 2).astype(BF16)

        h, hb, r = _ffn(h, hb, r, ffn1)

        hg = _matmul(hb, r, w_hg, head_major=True)
        lat = _matmul(hb, r, w_lat)
        o_h = _hgrn(hg, info, hgrn_lb_logits, hg_out_norm[l], l, hg_heads)

        o_a = []
        for (bsz, s, lp, off), tab in zip(trunks, tabs):
            q, k, v = _qkv_proj(lat, q_lat_norm[l].reshape(1, q_lora).astype(F32),
                                kv_lat_norm[l].reshape(1, kv_lora).astype(F32), wq, wkv,
                                _head_gain(q_head_norm[l]), _head_gain(k_head_norm[l]),
                                tab, off, bsz * lp, q_lora, kv_lora)
            o_a.append(_attention(q, k, v, bsz, lp, s + N_META))
        o_a = jnp.concatenate(o_a, axis=0)

        merged = _merge(hb, r, o_h, o_a, w_gh, w_ga, w_branch_hgrn[l].astype(BF16), w_branch_mla[l].astype(BF16))
        h, hb, r = _matmul_residual(merged, w_out[l].astype(BF16), h, 1.0)

        h, hb, r = _ffn(h, hb, r, ffn2)

    outs = []
    for (bsz, s, lp, off) in trunks:
        outs.append(jnp.stack([h[off + b * lp + N_META:off + b * lp + N_META + s] for b in range(bsz)]))
    return tuple(outs)
```

```python
import functools
import math

import jax
import jax.numpy as jnp
from jax import lax
from jax.experimental import pallas as pl
from jax.experimental.pallas import tpu as pltpu

N_META = 16
HG_HEAD_DIM = 128
QK_NOPE = 128
QK_ROPE = 64
QK_HEAD = QK_NOPE + QK_ROPE
V_HEAD = 128
ROPE_THETA = 10000.0
RMS_EPS = 1e-6

LANES = 128
SEQ_PAD = 128
HG_CHUNK_ROWS = 128
ROW_TILES = (1024, 512, 256, 128, 64, 32, 16)
HG_SUB_ROWS = 32
HG_EXACT_SUB_ROWS = 16
HG_MAX_LOG2_GROWTH = 115.0
ATTN_MAX_Q_ROWS = 640
ATTN_KEY_BLOCKS = 3
QK_PAD = 256
VMEM_LIMIT = 56 * 1024 * 1024
VMEM_LIMIT_LARGE = 58 * 1024 * 1024
F32 = jnp.float32
BF16 = jnp.bfloat16
NEG_BIG = -1e30
LN2 = math.log(2.0)
LOG2E = 1.0 / LN2


def _cparams(sem, vmem_limit=VMEM_LIMIT):
    return pltpu.CompilerParams(dimension_semantics=sem, vmem_limit_bytes=vmem_limit)


def _pick(n, prefs):
    for p in prefs:
        if n % p == 0:
            return p
    return n


def _sigmoid(x):
    return 1.0 / (1.0 + jnp.exp(-x))


def _row_scale_kernel(x_ref, w_ref, hb_ref, r_ref):
    x = x_ref[...]
    hb_ref[...] = (x * w_ref[...]).astype(BF16)
    r_ref[...] = lax.rsqrt(jnp.mean(x * x, axis=-1, keepdims=True) + RMS_EPS)


def _row_scale(x, w):
    m, d = x.shape
    tm = _pick(m, (512, 256, 128, 64, 32, 16, 8))
    return pl.pallas_call(
        _row_scale_kernel,
        grid=(m // tm,),
        in_specs=[pl.BlockSpec((tm, d), lambda i: (i, 0)),
                  pl.BlockSpec((1, d), lambda i: (0, 0))],
        out_specs=[pl.BlockSpec((tm, d), lambda i: (i, 0)),
                   pl.BlockSpec((tm, 1), lambda i: (i, 0))],
        out_shape=[jax.ShapeDtypeStruct((m, d), BF16), jax.ShapeDtypeStruct((m, 1), F32)],
        compiler_params=_cparams(("parallel",)),
        name="row_scale",
    )(x, w)


def _mm_kernel(a_ref, r_ref, b_ref, o_ref):
    o_ref[...] = jnp.dot(a_ref[...], b_ref[...], preferred_element_type=F32) * r_ref[...]


def _mm_headmajor_kernel(a_ref, r_ref, b_ref, o_ref):
    acc = jnp.dot(a_ref[...], b_ref[...], preferred_element_type=F32) * r_ref[...]
    for j in range(o_ref.shape[0]):
        o_ref[j] = acc[:, j * LANES:(j + 1) * LANES]


def _matmul(a, r, b, *, head_major=False):
    m, k = a.shape
    n = b.shape[1]
    if n % 256 == 0:
        tm = _pick(m, ROW_TILES)
        tn = _pick(n, (512, 256))
    else:
        tm = _pick(m, (512, 256, 128, 64, 32, 16))
        tn = n
    grid = (m // tm, n // tn)
    in_specs = [pl.BlockSpec((tm, k), lambda i, j: (i, 0)),
                pl.BlockSpec((tm, 1), lambda i, j: (i, 0)),
                pl.BlockSpec((k, tn), lambda i, j: (0, j))]
    if head_major:
        return pl.pallas_call(
            _mm_headmajor_kernel, grid=grid, in_specs=in_specs,
            out_specs=pl.BlockSpec((tn // LANES, tm, LANES), lambda i, j: (j, i, 0)),
            out_shape=jax.ShapeDtypeStruct((n // LANES, m, LANES), F32),
            compiler_params=_cparams(("parallel", "arbitrary")),
            name="proj_headmajor",
        )(a, r, b)
    return pl.pallas_call(
        _mm_kernel, grid=grid, in_specs=in_specs,
        out_specs=pl.BlockSpec((tm, tn), lambda i, j: (i, j)),
        out_shape=jax.ShapeDtypeStruct((m, n), F32),
        compiler_params=_cparams(("parallel", "arbitrary")),
        name="proj",
    )(a, r, b)


def _ffn_in_kernel(x_ref, r_ref, wg_ref, wu_ref, o_ref):
    x = x_ref[...]
    r = r_ref[...]
    g = jnp.dot(x, wg_ref[...], preferred_element_type=F32) * r
    u = jnp.dot(x, wu_ref[...], preferred_element_type=F32) * r
    o_ref[...] = ((g * _sigmoid(g)) * u).astype(o_ref.dtype)


def _ffn_in(x, r, wg, wu):
    m, k = x.shape
    f = wg.shape[1]
    tm = _pick(m, ROW_TILES)
    tf = min(512, f)
    return pl.pallas_call(
        _ffn_in_kernel,
        grid=(m // tm, pl.cdiv(f, tf)),
        in_specs=[pl.BlockSpec((tm, k), lambda i, j: (i, 0)),
                  pl.BlockSpec((tm, 1), lambda i, j: (i, 0)),
                  pl.BlockSpec((k, tf), lambda i, j: (0, j)),
                  pl.BlockSpec((k, tf), lambda i, j: (0, j))],
        out_specs=pl.BlockSpec((tm, tf), lambda i, j: (i, j)),
        out_shape=jax.ShapeDtypeStruct((m, f), BF16),
        compiler_params=_cparams(("parallel", "arbitrary")),
        name="ffn_in",
    )(x, r, wg, wu)


def _mm_res_kernel(a_ref, b_ref, res_ref, w_ref, o_ref, hb_ref, rs_ref, ss_ref, *, scale, k_tail):
    j, kk = pl.program_id(1), pl.program_id(2)

    @pl.when(kk == 0)
    def _():
        o_ref[...] = res_ref[...]

    last = pl.num_programs(2) - 1

    @pl.when(kk < last)
    def _():
        o_ref[...] += scale * jnp.dot(a_ref[...], b_ref[...], preferred_element_type=F32)

    @pl.when(kk == last)
    def _():
        a, b = a_ref[...], b_ref[...]
        if k_tail:
            a = jnp.where(lax.broadcasted_iota(jnp.int32, a.shape, 1) < k_tail, a, jnp.zeros_like(a))
            b = jnp.where(lax.broadcasted_iota(jnp.int32, b.shape, 0) < k_tail, b, jnp.zeros_like(b))
        o = o_ref[...] + scale * jnp.dot(a, b, preferred_element_type=F32)
        o_ref[...] = o
        hb_ref[...] = (o * w_ref[...]).astype(BF16)
        part = jnp.sum(o * o, axis=-1, keepdims=True)
        ss = jnp.where(j == 0, part, ss_ref[...] + part)
        ss_ref[...] = ss
        d_model = o.shape[1] * pl.num_programs(1)
        rs_ref[...] = lax.rsqrt(ss * (1.0 / d_model) + RMS_EPS)


def _matmul_residual(a, b, res, scale, next_gain):
    m, k = a.shape
    n = b.shape[1]
    tm = _pick(m, ROW_TILES)
    tn = _pick(n, (2048, 1024, 512, 256, 128))
    tk = min(1024, k)
    return pl.pallas_call(
        functools.partial(_mm_res_kernel, scale=scale, k_tail=k % tk),
        grid=(m // tm, n // tn, pl.cdiv(k, tk)),
        in_specs=[pl.BlockSpec((tm, tk), lambda i, j, kk: (i, kk)),
                  pl.BlockSpec((tk, tn), lambda i, j, kk: (kk, j)),
                  pl.BlockSpec((tm, tn), lambda i, j, kk: (i, j)),
                  pl.BlockSpec((1, tn), lambda i, j, kk: (0, j))],
        out_specs=[pl.BlockSpec((tm, tn), lambda i, j, kk: (i, j)),
                   pl.BlockSpec((tm, tn), lambda i, j, kk: (i, j)),
                   pl.BlockSpec((tm, 1), lambda i, j, kk: (i, 0))],
        out_shape=[jax.ShapeDtypeStruct((m, n), F32), jax.ShapeDtypeStruct((m, n), BF16),
                   jax.ShapeDtypeStruct((m, 1), F32)],
        scratch_shapes=[pltpu.VMEM((tm, 1), F32)],
        compiler_params=_cparams(("parallel", "arbitrary", "arbitrary"), VMEM_LIMIT_LARGE),
        name="proj_residual",
    )(a, b, res, next_gain.reshape(1, n).astype(F32))


def _merge_kernel(u_ref, r_ref, oh_ref, oa_ref, wgh_ref, wga_ref, wbh_ref, wba_ref, o_ref):
    u = u_ref[...]
    r = r_ref[...]
    oh = jnp.concatenate([oh_ref[h] for h in range(oh_ref.shape[0])], axis=1)
    gh = jnp.dot(u, wgh_ref[...], preferred_element_type=F32) * r
    ga = jnp.dot(u, wga_ref[...], preferred_element_type=F32) * r
    bh = jnp.dot(oh, wbh_ref[...], preferred_element_type=F32)
    ba = jnp.dot(oa_ref[...], wba_ref[...], preferred_element_type=F32)
    o_ref[...] = (_sigmoid(gh) * bh + _sigmoid(ga) * ba).astype(o_ref.dtype)


def _merge(u, r, oh, oa, wgh, wga, wbh, wba):
    m, d = u.shape
    nh = oh.shape[0]
    n = wgh.shape[1]
    tm = _pick(m, ROW_TILES)
    tn = _pick(n, (256, 128))
    return pl.pallas_call(
        _merge_kernel,
        grid=(m // tm, n // tn),
        in_specs=[pl.BlockSpec((tm, d), lambda i, j: (i, 0)),
                  pl.BlockSpec((tm, 1), lambda i, j: (i, 0)),
                  pl.BlockSpec((nh, tm, LANES), lambda i, j: (0, i, 0)),
                  pl.BlockSpec((tm, oa.shape[1]), lambda i, j: (i, 0)),
                  pl.BlockSpec((d, tn), lambda i, j: (0, j)),
                  pl.BlockSpec((d, tn), lambda i, j: (0, j)),
                  pl.BlockSpec((wbh.shape[0], tn), lambda i, j: (0, j)),
                  pl.BlockSpec((wba.shape[0], tn), lambda i, j: (0, j))],
        out_specs=pl.BlockSpec((tm, tn), lambda i, j: (i, j)),
        out_shape=jax.ShapeDtypeStruct((m, n), BF16),
        compiler_params=_cparams(("parallel", "arbitrary")),
        name="branch_merge",
    )(u, r, oh, oa, wgh, wga, wbh, wba)


def _rope_tail(y2, tab):
    w = y2 * tab
    rot = w + pltpu.roll(w, QK_ROPE, axis=1)
    lane = lax.broadcasted_iota(jnp.int32, rot.shape, 1)
    return jnp.where(lane < QK_ROPE, rot, 0.0)


def _rope_sumsq(y2):
    lane = lax.broadcasted_iota(jnp.int32, y2.shape, 1)
    y2m = jnp.where(lane < QK_ROPE, y2, 0.0)
    return jnp.sum(y2m * y2m, axis=-1, keepdims=True)


def _head_rms_scale(y, ones_mask, extra_ss):
    ss = jnp.dot((y * y).astype(BF16), ones_mask, preferred_element_type=F32)
    return lax.rsqrt((ss + extra_ss) * (1.0 / QK_HEAD) + RMS_EPS)


def _latent_norm(x, w):
    ms = jnp.mean(x * x, axis=-1, keepdims=True)
    return ((x * lax.rsqrt(ms + RMS_EPS)) * w).astype(BF16)


def _qkv_kernel(lat_ref, qlw_ref, kvlw_ref, wq_ref, wkv_ref, gq_ref, gk_ref, tab_ref,
                q_ref, k_ref, v_ref, *, q_lora, kv_lora, q_scale):
    xq = _latent_norm(lat_ref[:, 0:q_lora], qlw_ref[...])
    xkv = _latent_norm(lat_ref[:, q_lora:q_lora + kv_lora], kvlw_ref[...])
    kr = lat_ref[:, q_lora + kv_lora:q_lora + kv_lora + LANES]
    gq1 = gq_ref[:, :QK_NOPE] * q_scale
    gk1 = gk_ref[:, :QK_NOPE]
    tab_q = tab_ref[...] * (gq_ref[:, QK_NOPE:] * q_scale)
    tab_k = tab_ref[...] * gk_ref[:, QK_NOPE:]
    kr_ss = _rope_sumsq(kr)
    row = lax.broadcasted_iota(jnp.int32, (QK_PAD, LANES), 0)
    ones_q = jnp.where(row < QK_HEAD, 1.0, 0.0).astype(BF16)
    ones_k = jnp.ones((QK_NOPE, LANES), BF16)

    def body(h, carry):
        y = jnp.dot(xq, wq_ref[h], preferred_element_type=F32)
        y1, y2 = y[:, :QK_NOPE], y[:, QK_NOPE:]
        r = _head_rms_scale(y, ones_q, 0.0)
        q_ref[h, :, 0:QK_NOPE] = ((y1 * r) * gq1).astype(BF16)
        q_ref[h, :, QK_NOPE:] = _rope_tail(y2 * r, tab_q).astype(BF16)

        z = jnp.dot(xkv, wkv_ref[h], preferred_element_type=F32)
        z1 = z[:, :QK_NOPE]
        rk = _head_rms_scale(z1, ones_k, kr_ss)
        k_ref[h, :, 0:QK_NOPE] = ((z1 * rk) * gk1).astype(BF16)
        k_ref[h, :, QK_NOPE:] = _rope_tail(kr * rk, tab_k).astype(BF16)
        v_ref[h] = z[:, QK_NOPE:].astype(BF16)
        return carry

    lax.fori_loop(0, q_ref.shape[0], body, 0, unroll=2)


def _qkv_proj(lat, q_lat_w, kv_lat_w, wq, wkv, gq, gk, tab, row_off, rows, q_lora, kv_lora):
    nh = wq.shape[0]
    tl = _pick(math.gcd(rows, row_off) if row_off else rows, (512, 256, 128, 64, 32, 16))
    off = row_off // tl
    full = lambda shape: pl.BlockSpec(shape, lambda i: (0,) * len(shape))
    return pl.pallas_call(
        functools.partial(_qkv_kernel, q_lora=q_lora, kv_lora=kv_lora,
                          q_scale=LOG2E / math.sqrt(QK_HEAD)),
        grid=(rows // tl,),
        in_specs=[pl.BlockSpec((tl, lat.shape[1]), lambda i: (off + i, 0)),
                  full((1, q_lora)), full((1, kv_lora)),
                  full((nh, q_lora, QK_PAD)), full((nh, kv_lora, QK_NOPE + V_HEAD)),
                  full((1, QK_PAD)), full((1, QK_PAD)),
                  pl.BlockSpec((tl, LANES), lambda i: (i, 0))],
        out_specs=[pl.BlockSpec((nh, tl, QK_PAD), lambda i: (0, i, 0)),
                   pl.BlockSpec((nh, tl, QK_PAD), lambda i: (0, i, 0)),
                   pl.BlockSpec((nh, tl, V_HEAD), lambda i: (0, i, 0))],
        out_shape=[jax.ShapeDtypeStruct((nh, rows, QK_PAD), BF16),
                   jax.ShapeDtypeStruct((nh, rows, QK_PAD), BF16),
                   jax.ShapeDtypeStruct((nh, rows, V_HEAD), BF16)],
        compiler_params=_cparams(("parallel",)),
        name="mla_qkv_proj",
    )(lat, q_lat_w, kv_lat_w, wq, wkv, gq, gk, tab)


def _attn_kernel(q_ref, k_ref, v_ref, o_ref, *, n_valid):
    q = q_ref[0]
    lp = k_ref.shape[1]
    kb = -(-n_valid // (ATTN_KEY_BLOCKS * QK_PAD)) * QK_PAD
    edges = list(range(0, min(lp, n_valid), kb)) + [min(lp, -(-n_valid // LANES) * LANES)]
    m = denom = acc = None
    for c0, c1 in zip(edges[:-1], edges[1:]):
        s = lax.dot_general(q, k_ref[0, c0:c1, :], (((1,), (1,)), ((), ())), preferred_element_type=F32)
        if c1 > n_valid:
            col = lax.broadcasted_iota(jnp.int32, s.shape, 1) + c0
            s = jnp.where(col < n_valid, s, NEG_BIG)
        mb = jnp.max(s, axis=-1, keepdims=True)
        if m is None:
            m = mb
            p = jnp.exp2(s - m)
            denom = jnp.sum(p, axis=-1, keepdims=True)
            acc = jnp.dot(p.astype(BF16), v_ref[0, c0:c1, :], preferred_element_type=F32)
        else:
            m_new = jnp.maximum(m, mb)
            alpha = jnp.exp2(m - m_new)
            p = jnp.exp2(s - m_new)
            denom = alpha * denom + jnp.sum(p, axis=-1, keepdims=True)
            acc = alpha * acc + jnp.dot(p.astype(BF16), v_ref[0, c0:c1, :], preferred_element_type=F32)
            m = m_new
    o_ref[...] = (acc / denom).astype(o_ref.dtype)


def _attention(q, k, v, batch, lp, n_valid):
    nh = q.shape[0]
    nq = next(n for n in range(1, lp + 1) if lp % n == 0 and (lp // n) % 16 == 0 and lp // n <= ATTN_MAX_Q_ROWS)
    tq = lp // nq
    return pl.pallas_call(
        functools.partial(_attn_kernel, n_valid=n_valid),
        grid=(batch, nh, nq),
        in_specs=[pl.BlockSpec((1, tq, QK_PAD), lambda b, h, i: (h, b * nq + i, 0)),
                  pl.BlockSpec((1, lp, QK_PAD), lambda b, h, i: (h, b, 0)),
                  pl.BlockSpec((1, lp, V_HEAD), lambda b, h, i: (h, b, 0))],
        out_specs=pl.BlockSpec((tq, V_HEAD), lambda b, h, i: (b * nq + i, h)),
        out_shape=jax.ShapeDtypeStruct((batch * lp, nh * V_HEAD), BF16),
        compiler_params=_cparams(("parallel", "parallel", "arbitrary")),
        name="mla_attention",
    )(q, k, v)


def _lower_bound(logit_ref, h, layer):
    lg = logit_ref[0, h]
    e = jnp.exp(lg - jnp.max(lg, axis=0, keepdims=True))
    p = e / jnp.sum(e, axis=0, keepdims=True)
    lb = jnp.zeros((1, lg.shape[1]), F32)
    for j in range(1, layer + 1):
        lb = lb + p[j:j + 1]
    return lb


def _gates(zq, zf, zi, lb, row_ok):
    q = zq * _sigmoid(zq)
    e = jnp.exp(-jnp.abs(zf))
    r = 1.0 / (1.0 + e)
    er = e * r
    pos = zf >= 0.0
    k = jnp.where(row_ok, (1.0 - lb) * jnp.where(pos, er, r), 0.0)
    f = lb + (1.0 - lb) * jnp.where(pos, r, er)
    g = jnp.maximum(jnp.log(f) * LOG2E, (jnp.log1p(-lb) * LOG2E - 1.0) + jnp.minimum(zf, 0.0) * LOG2E)
    return q, k, zi, g


def _log_cumsum(g, tri):
    g1 = g.astype(BF16)
    r1 = g - g1.astype(F32)
    g2 = r1.astype(BF16)
    g3 = (r1 - g2.astype(F32)).astype(BF16)
    bb = jnp.dot(tri, jnp.concatenate([g1, g2, g3], axis=1), preferred_element_type=F32)
    return (bb[:, :LANES] + bb[:, LANES:2 * LANES]) + bb[:, 2 * LANES:]


def _carried(q, k, v, b, total, st):
    qh = (q * jnp.exp2(b)).astype(BF16)
    kh = (k * jnp.exp2(total - b)).astype(BF16)
    o = lax.dot_general(qh, st.astype(BF16), (((1,), (1,)), ((), ())), preferred_element_type=F32)
    st_new = st * jnp.exp2(total) + lax.dot_general(
        v.astype(BF16), kh, (((0,), (0,)), ((), ())), preferred_element_type=F32)
    return o, st_new


def _hgrn_chunk_factorised(q, k, v, g, st, tri, in_scan, reverse):
    c_rows = q.shape[0]
    sub = HG_SUB_ROWS
    b = _log_cumsum(g, tri)
    total = b[0:1] if reverse else b[c_rows - 1:c_rows]
    o, st_new = _carried(q, k, v, b, total, st)

    zero_row = jnp.zeros((1, LANES), F32)
    growth = zero_row
    rows = []
    for blk in range(c_rows // sub):
        lo, hi = blk * sub, (blk + 1) * sub
        if reverse:
            ref = b[hi:hi + 1] if hi < c_rows else zero_row
            growth = jnp.maximum(growth, ref - b[lo:lo + 1])
        else:
            ref = b[lo - 1:lo] if lo > 0 else zero_row
            growth = jnp.maximum(growth, ref - b[hi - 1:hi])
        qi = (q[lo:hi] * jnp.exp2(b[lo:hi] - ref)).astype(BF16)
        ks = (k * jnp.exp2(jnp.minimum(ref - b, HG_MAX_LOG2_GROWTH))).astype(BF16)
        rows.append(lax.dot_general(qi, ks, (((1,), (1,)), ((), ())), preferred_element_type=F32))
    a = jnp.where(in_scan, jnp.concatenate(rows, axis=0), 0.0)
    o = o + jnp.dot(a.astype(BF16), v.astype(BF16), preferred_element_type=F32)
    return o, st_new, growth


def _hgrn_chunk_exact(q, k, v, g, st, tri, ones_sq, reverse):
    c_rows = q.shape[0]
    sub = HG_EXACT_SUB_ROWS
    nb = c_rows // sub
    b = _log_cumsum(g, tri)
    total = b[0:1] if reverse else b[c_rows - 1:c_rows]
    o, _ = _carried(q, k, v, b, total, st)

    vb = v.astype(BF16)
    pieces = []
    for blk in range(nb):
        lo, hi = blk * sub, (blk + 1) * sub
        if reverse:
            if blk == nb - 1:
                pieces.append(jnp.zeros((sub, LANES), F32))
                continue
            ref = b[hi:hi + 1]
            other = slice(hi, c_rows)
        else:
            if blk == 0:
                pieces.append(jnp.zeros((sub, LANES), F32))
                continue
            ref = b[lo - 1:lo]
            other = slice(0, lo)
        qi = (q[lo:hi] * jnp.exp2(b[lo:hi] - ref)).astype(BF16)
        ko = (k[other] * jnp.exp2(ref - b[other])).astype(BF16)
        a = lax.dot_general(qi, ko, (((1,), (1,)), ((), ())), preferred_element_type=F32)
        pieces.append(jnp.dot(a.astype(BF16), vb[other], preferred_element_type=F32))
    o = o + jnp.concatenate(pieces, axis=0)

    pos = lax.broadcasted_iota(jnp.int32, (c_rows, LANES), 0) % sub
    d_list, v_list = [], []
    for d in range(sub):
        if d == 0:
            vs = v
            dmat = q * k
        else:
            shift = (c_rows - d) if reverse else d
            ks = pltpu.roll(k, shift, axis=0)
            bs = pltpu.roll(b, shift, axis=0)
            vs = pltpu.roll(v, shift, axis=0)
            ok = (pos + d < sub) if reverse else (pos >= d)
            dmat = (q * ks) * jnp.exp2(jnp.where(ok, b - bs, NEG_BIG))
        d_list.append(dmat.astype(BF16))
        v_list.append(vs)
    rs = jnp.dot(jnp.concatenate(d_list, axis=0), ones_sq, preferred_element_type=F32)
    for d in range(sub):
        o = o + rs[d * c_rows:(d + 1) * c_rows] * v_list[d]
    return o


def _hgrn_kernel(info_ref, *refs, layer, reverse):
    if reverse:
        zq_ref, zf_ref, zi_ref, zg_ref, of_ref, lg_ref, nw_ref, o_ref, st_ref = refs
    else:
        zq_ref, zf_ref, zi_ref, lg_ref, o_ref, st_ref = refs
    step = pl.program_id(0)
    g = (pl.num_programs(0) - 1 - step) if reverse else step
    local, n_chunks, seq_rows = info_ref[3 * g], info_ref[3 * g + 1], info_ref[3 * g + 2]
    nh, c_rows = zq_ref.shape[0], zq_ref.shape[1]
    cur = step % 2

    @pl.when((local == n_chunks - 1) if reverse else (local == 0))
    def _():
        st_ref[cur] = jnp.zeros(st_ref.shape[1:], F32)

    rr = lax.broadcasted_iota(jnp.int32, (c_rows, c_rows), 0)
    cc = lax.broadcasted_iota(jnp.int32, (c_rows, c_rows), 1)
    in_scan = (cc >= rr) if reverse else (cc <= rr)
    tri = jnp.where(in_scan, 1.0, 0.0).astype(BF16)
    row_ok = (lax.broadcasted_iota(jnp.int32, (c_rows, LANES), 0) + local * c_rows) < seq_rows

    def finish(h, o):
        if reverse:
            o = o + of_ref[h]
            ms = jnp.mean(o * o, axis=-1, keepdims=True)
            zg = zg_ref[h]
            o_ref[h] = (((o * lax.rsqrt(ms + RMS_EPS)) * nw_ref[...]) * (zg * _sigmoid(zg))).astype(o_ref.dtype)
        else:
            o_ref[h] = o

    def fast_body(h, growth):
        q, k, v, gl = _gates(zq_ref[h], zf_ref[h], zi_ref[h], _lower_bound(lg_ref, h, layer), row_ok)
        o, st_new, gr = _hgrn_chunk_factorised(q, k, v, gl, st_ref[cur, h], tri, in_scan, reverse)
        st_ref[1 - cur, h] = st_new
        finish(h, o)
        return jnp.maximum(growth, gr)

    growth = lax.fori_loop(0, nh, fast_body, jnp.zeros((1, LANES), F32), unroll=16)

    @pl.when(jnp.max(growth) > HG_MAX_LOG2_GROWTH)
    def _():
        ones_sq = jnp.ones((LANES, LANES), BF16)

        def exact_body(h, carry):
            q, k, v, gl = _gates(zq_ref[h], zf_ref[h], zi_ref[h], _lower_bound(lg_ref, h, layer), row_ok)
            finish(h, _hgrn_chunk_exact(q, k, v, gl, st_ref[cur, h], tri, ones_sq, reverse))
            return carry

        lax.fori_loop(0, nh, exact_body, 0)


def _hgrn(hg, info, lb_logits, out_norm, layer, nh):
    m = hg.shape[1]
    c = HG_CHUNK_ROWS
    n_chunks = m // c
    depth = lb_logits.shape[1]
    lg = lb_logits.astype(F32).reshape(2, depth, nh, LANES).transpose(0, 2, 1, 3)

    def blk(grp, rev):
        if rev:
            return pl.BlockSpec((nh, c, LANES), lambda g, info_ref: (grp, n_chunks - 1 - g, 0))
        return pl.BlockSpec((nh, c, LANES), lambda g, info_ref: (grp, g, 0))

    state = pltpu.VMEM((2, nh, LANES, LANES), F32)
    o_f = pl.pallas_call(
        functools.partial(_hgrn_kernel, layer=layer, reverse=False),
        grid_spec=pltpu.PrefetchScalarGridSpec(
            num_scalar_prefetch=1, grid=(n_chunks,),
            in_specs=[blk(0, False), blk(1, False), blk(3, False),
                      pl.BlockSpec((1, nh, depth, LANES), lambda g, info_ref: (0, 0, 0, 0))],
            out_specs=blk(0, False),
            scratch_shapes=[state]),
        out_shape=jax.ShapeDtypeStruct((nh, m, LANES), F32),
        compiler_params=_cparams(("arbitrary",)),
        name="hgrn_forward",
    )(info, hg, hg, hg, lg)

    return pl.pallas_call(
        functools.partial(_hgrn_kernel, layer=layer, reverse=True),
        grid_spec=pltpu.PrefetchScalarGridSpec(
            num_scalar_prefetch=1, grid=(n_chunks,),
            in_specs=[blk(0, True), blk(2, True), blk(3, True), blk(4, True), blk(0, True),
                      pl.BlockSpec((1, nh, depth, LANES), lambda g, info_ref: (1, 0, 0, 0)),
                      pl.BlockSpec((1, LANES), lambda g, info_ref: (0, 0))],
            out_specs=blk(0, True),
            scratch_shapes=[state]),
        out_shape=jax.ShapeDtypeStruct((nh, m, LANES), BF16),
        compiler_params=_cparams(("arbitrary",)),
        name="hgrn_backward",
    )(info, hg, hg, hg, hg, o_f, lg, out_norm.reshape(1, LANES).astype(F32))


def _swap_halves(x, axis):
    a, b = jnp.split(x, 2, axis=axis)
    return jnp.concatenate([b, a], axis=axis)


def _rope_table(rows):
    inv_freq = 1.0 / (ROPE_THETA ** (jnp.arange(0, QK_ROPE, 2, dtype=F32) / QK_ROPE))
    ang = jnp.arange(rows, dtype=F32)[:, None] * inv_freq[None, :]
    c, s = jnp.cos(ang), jnp.sin(ang)
    return jnp.concatenate([c, c, -s, s], axis=1)


def _head_gain(g):
    g = g.astype(F32)
    return jnp.concatenate([g[:QK_NOPE], g[QK_NOPE:], _swap_halves(g[QK_NOPE:], 0)]).reshape(1, QK_PAD)


def _ffn_weights(w_in, w_out):
    d_ff = w_out.shape[0]
    return w_in[:, :d_ff].astype(BF16), w_in[:, d_ff:].astype(BF16), w_out.astype(BF16)


def _ffn(h, hb, r, weights, next_gain, scale=0.5):
    wg, wu, wo = weights
    return _matmul_residual(_ffn_in(hb, r, wg, wu), wo, h, scale, next_gain)


def kernel(x_prompt, x_sample, meta_tokens, hgrn_lb_logits, norm_ffn1, w_ffn1_in, w_ffn1_out, norm_mix, w_in,
           q_lat_norm, w_uq, kv_lat_norm, w_ukv, q_head_norm, k_head_norm, hg_out_norm,
           w_branch_hgrn, w_branch_mla, w_out, norm_ffn2, w_ffn2_in, w_ffn2_out):
    depth = norm_ffn1.shape[0]
    d_model = x_prompt.shape[-1]
    hg_dim = hgrn_lb_logits.shape[-1]
    hg_heads = hg_dim // HG_HEAD_DIM
    q_lora = q_lat_norm.shape[-1]
    kv_lora = kv_lat_norm.shape[-1]
    mla_heads = w_uq.shape[-1] // QK_HEAD

    trunks = []
    row_off = 0
    blocks, info = [], []
    meta = meta_tokens.astype(F32)
    pad = jnp.zeros((SEQ_PAD - N_META, d_model), F32)
    for x in (x_prompt, x_sample):
        bsz, s, _ = x.shape
        lp = s + SEQ_PAD
        for b in range(bsz):
            blocks += [meta, x[b].astype(F32), pad]
        nc = lp // HG_CHUNK_ROWS
        for _ in range(bsz):
            for c in range(nc):
                info += [c, nc, s + N_META]
        trunks.append((bsz, s, lp, row_off))
        row_off += bsz * lp
    h = jnp.concatenate(blocks, axis=0)
    hb, r = _row_scale(h, norm_ffn1[0].reshape(1, d_model).astype(F32))
    info = jnp.asarray(info, jnp.int32)
    tabs = [jnp.tile(_rope_table(lp), (bsz, 1)) for (bsz, s, lp, _) in trunks]

    sizes = (hg_dim,) * 5 + (q_lora, kv_lora, QK_ROPE, d_model, d_model)
    offs = [0]
    for sz in sizes:
        offs.append(offs[-1] + sz)

    for l in range(depth):
        ffn1 = _ffn_weights(w_ffn1_in[l], w_ffn1_out[l])
        ffn2 = _ffn_weights(w_ffn2_in[l], w_ffn2_out[l])
        wl = w_in[l]
        w_hg = wl[:, :offs[5]].astype(BF16)
        w_kr = wl[:, offs[7]:offs[8]]
        w_lat = jnp.concatenate([wl[:, offs[5]:offs[7]], w_kr, _swap_halves(w_kr, 1)], axis=1).astype(BF16)
        w_gh = wl[:, offs[8]:offs[9]].astype(BF16)
        w_ga = wl[:, offs[9]:offs[10]].astype(BF16)
        after_ffn2 = norm_ffn1[l + 1] if l + 1 < depth else jnp.ones((d_model,), F32)
        wq = w_uq[l].reshape(q_lora, mla_heads, QK_HEAD)
        wq = jnp.concatenate([wq, _swap_halves(wq[:, :, QK_NOPE:], 2)], axis=2)
        wq = wq.transpose(1, 0, 2).astype(BF16)
        wkv = w_ukv[l].reshape(kv_lora, mla_heads, QK_NOPE + V_HEAD).transpose(1, 0, 2).astype(BF16)

        h, hb, r = _ffn(h, hb, r, ffn1, norm_mix[l])

        hg = _matmul(hb, r, w_hg, head_major=True)
        lat = _matmul(hb, r, w_lat)
        o_h = _hgrn(hg, info, hgrn_lb_logits, hg_out_norm[l], l, hg_heads)

        o_a = []
        for (bsz, s, lp, off), tab in zip(trunks, tabs):
            q, k, v = _qkv_proj(lat, q_lat_norm[l].reshape(1, q_lora).astype(F32),
                                kv_lat_norm[l].reshape(1, kv_lora).astype(F32), wq, wkv,
                                _head_gain(q_head_norm[l]), _head_gain(k_head_norm[l]),
                                tab, off, bsz * lp, q_lora, kv_lora)
            o_a.append(_attention(q, k, v, bsz, lp, s + N_META))
        o_a = jnp.concatenate(o_a, axis=0)

        merged = _merge(hb, r, o_h, o_a, w_gh, w_ga, w_branch_hgrn[l].astype(BF16), w_branch_mla[l].astype(BF16))
        h, hb, r = _matmul_residual(merged, w_out[l].astype(BF16), h, 1.0, norm_ffn2[l])

        h, hb, r = _ffn(h, hb, r, ffn2, after_ffn2)

    outs = []
    for (bsz, s, lp, off) in trunks:
        outs.append(jnp.stack([h[off + b * lp + N_META:off + b * lp + N_META + s] for b in range(bsz)]))
    return tuple(outs)
```

```python
import functools
import math

import jax
import jax.numpy as jnp
from jax import lax
from jax.experimental import pallas as pl
from jax.experimental.pallas import tpu as pltpu

N_META = 16
HG_HEAD_DIM = 128
QK_NOPE = 128
QK_ROPE = 64
QK_HEAD = QK_NOPE + QK_ROPE
V_HEAD = 128
ROPE_THETA = 10000.0
RMS_EPS = 1e-6

LANES = 128
SEQ_PAD = 128
HG_CHUNK_ROWS = 128
ROW_TILES = (1024, 512, 256, 128, 64, 32, 16)
HG_SUB_ROWS = 32
HG_EXACT_SUB_ROWS = 16
HG_MAX_LOG2_GROWTH = 115.0
ATTN_MAX_Q_ROWS = 640
QK_PAD = 256
VMEM_LIMIT = 56 * 1024 * 1024
VMEM_LIMIT_LARGE = 58 * 1024 * 1024
F32 = jnp.float32
BF16 = jnp.bfloat16
NEG_BIG = -1e30
LN2 = math.log(2.0)
LOG2E = 1.0 / LN2


def _cparams(sem, vmem_limit=VMEM_LIMIT):
    return pltpu.CompilerParams(dimension_semantics=sem, vmem_limit_bytes=vmem_limit)


def _pick(n, prefs):
    for p in prefs:
        if n % p == 0:
            return p
    return n


def _sigmoid(x):
    return 1.0 / (1.0 + jnp.exp(-x))


def _row_scale_kernel(x_ref, w_ref, hb_ref, r_ref):
    x = x_ref[...]
    hb_ref[...] = (x * w_ref[...]).astype(BF16)
    r_ref[...] = lax.rsqrt(jnp.mean(x * x, axis=-1, keepdims=True) + RMS_EPS)


def _row_scale(x, w):
    m, d = x.shape
    tm = _pick(m, (512, 256, 128, 64, 32, 16, 8))
    return pl.pallas_call(
        _row_scale_kernel,
        grid=(m // tm,),
        in_specs=[pl.BlockSpec((tm, d), lambda i: (i, 0)),
                  pl.BlockSpec((1, d), lambda i: (0, 0))],
        out_specs=[pl.BlockSpec((tm, d), lambda i: (i, 0)),
                   pl.BlockSpec((tm, 1), lambda i: (i, 0))],
        out_shape=[jax.ShapeDtypeStruct((m, d), BF16), jax.ShapeDtypeStruct((m, 1), F32)],
        compiler_params=_cparams(("parallel",)),
        name="row_scale",
    )(x, w)


def _mm_kernel(a_ref, r_ref, b_ref, o_ref):
    o_ref[...] = jnp.dot(a_ref[...], b_ref[...], preferred_element_type=F32) * r_ref[...]


def _mm_headmajor_kernel(a_ref, r_ref, b_ref, o_ref):
    acc = jnp.dot(a_ref[...], b_ref[...], preferred_element_type=F32) * r_ref[...]
    for j in range(o_ref.shape[0]):
        o_ref[j] = acc[:, j * LANES:(j + 1) * LANES]


def _matmul(a, r, b, *, head_major=False):
    m, k = a.shape
    n = b.shape[1]
    if n % 256 == 0:
        tm = _pick(m, ROW_TILES)
        tn = _pick(n, (1024, 512, 256))
    else:
        tm = _pick(m, (512, 256, 128, 64, 32, 16))
        tn = n
    grid = (m // tm, n // tn)
    in_specs = [pl.BlockSpec((tm, k), lambda i, j: (i, 0)),
                pl.BlockSpec((tm, 1), lambda i, j: (i, 0)),
                pl.BlockSpec((k, tn), lambda i, j: (0, j))]
    if head_major:
        return pl.pallas_call(
            _mm_headmajor_kernel, grid=grid, in_specs=in_specs,
            out_specs=pl.BlockSpec((tn // LANES, tm, LANES), lambda i, j: (j, i, 0)),
            out_shape=jax.ShapeDtypeStruct((n // LANES, m, LANES), F32),
            compiler_params=_cparams(("parallel", "arbitrary")),
            name="proj_headmajor",
        )(a, r, b)
    return pl.pallas_call(
        _mm_kernel, grid=grid, in_specs=in_specs,
        out_specs=pl.BlockSpec((tm, tn), lambda i, j: (i, j)),
        out_shape=jax.ShapeDtypeStruct((m, n), F32),
        compiler_params=_cparams(("parallel", "arbitrary")),
        name="proj",
    )(a, r, b)


def _ffn_in_kernel(x_ref, r_ref, wg_ref, wu_ref, o_ref):
    x = x_ref[...]
    r = r_ref[...]
    g = jnp.dot(x, wg_ref[...], preferred_element_type=F32) * r
    u = jnp.dot(x, wu_ref[...], preferred_element_type=F32) * r
    o_ref[...] = ((g * _sigmoid(g)) * u).astype(o_ref.dtype)


def _ffn_in(x, r, wg, wu):
    m, k = x.shape
    f = wg.shape[1]
    tm = _pick(m, ROW_TILES)
    tf = min(512, f)
    return pl.pallas_call(
        _ffn_in_kernel,
        grid=(m // tm, pl.cdiv(f, tf)),
        in_specs=[pl.BlockSpec((tm, k), lambda i, j: (i, 0)),
                  pl.BlockSpec((tm, 1), lambda i, j: (i, 0)),
                  pl.BlockSpec((k, tf), lambda i, j: (0, j)),
                  pl.BlockSpec((k, tf), lambda i, j: (0, j))],
        out_specs=pl.BlockSpec((tm, tf), lambda i, j: (i, j)),
        out_shape=jax.ShapeDtypeStruct((m, f), BF16),
        compiler_params=_cparams(("parallel", "arbitrary")),
        name="ffn_in",
    )(x, r, wg, wu)


def _mm_res_kernel(a_ref, b_ref, res_ref, w_ref, o_ref, hb_ref, rs_ref, ss_ref, *, scale, k_tail):
    j, kk = pl.program_id(1), pl.program_id(2)

    @pl.when(kk == 0)
    def _():
        o_ref[...] = res_ref[...]

    last = pl.num_programs(2) - 1

    @pl.when(kk < last)
    def _():
        o_ref[...] += scale * jnp.dot(a_ref[...], b_ref[...], preferred_element_type=F32)

    @pl.when(kk == last)
    def _():
        a, b = a_ref[...], b_ref[...]
        if k_tail:
            a = jnp.where(lax.broadcasted_iota(jnp.int32, a.shape, 1) < k_tail, a, jnp.zeros_like(a))
            b = jnp.where(lax.broadcasted_iota(jnp.int32, b.shape, 0) < k_tail, b, jnp.zeros_like(b))
        o = o_ref[...] + scale * jnp.dot(a, b, preferred_element_type=F32)
        o_ref[...] = o
        hb_ref[...] = (o * w_ref[...]).astype(BF16)
        part = jnp.sum(o * o, axis=-1, keepdims=True)
        ss = jnp.where(j == 0, part, ss_ref[...] + part)
        ss_ref[...] = ss
        d_model = o.shape[1] * pl.num_programs(1)
        rs_ref[...] = lax.rsqrt(ss * (1.0 / d_model) + RMS_EPS)


def _matmul_residual(a, b, res, scale, next_gain):
    m, k = a.shape
    n = b.shape[1]
    tm = _pick(m, ROW_TILES)
    tn = _pick(n, (2048, 1024, 512, 256, 128))
    tk = min(1024, k)
    return pl.pallas_call(
        functools.partial(_mm_res_kernel, scale=scale, k_tail=k % tk),
        grid=(m // tm, n // tn, pl.cdiv(k, tk)),
        in_specs=[pl.BlockSpec((tm, tk), lambda i, j, kk: (i, kk)),
                  pl.BlockSpec((tk, tn), lambda i, j, kk: (kk, j)),
                  pl.BlockSpec((tm, tn), lambda i, j, kk: (i, j)),
                  pl.BlockSpec((1, tn), lambda i, j, kk: (0, j))],
        out_specs=[pl.BlockSpec((tm, tn), lambda i, j, kk: (i, j)),
                   pl.BlockSpec((tm, tn), lambda i, j, kk: (i, j)),
                   pl.BlockSpec((tm, 1), lambda i, j, kk: (i, 0))],
        out_shape=[jax.ShapeDtypeStruct((m, n), F32), jax.ShapeDtypeStruct((m, n), BF16),
                   jax.ShapeDtypeStruct((m, 1), F32)],
        scratch_shapes=[pltpu.VMEM((tm, 1), F32)],
        compiler_params=_cparams(("parallel", "arbitrary", "arbitrary"), VMEM_LIMIT_LARGE),
        name="proj_residual",
    )(a, b, res, next_gain.reshape(1, n).astype(F32))


def _merge_kernel(u_ref, r_ref, oh_ref, oa_ref, wgh_ref, wga_ref, wbh_ref, wba_ref, o_ref):
    u = u_ref[...]
    r = r_ref[...]
    oh = jnp.concatenate([oh_ref[h] for h in range(oh_ref.shape[0])], axis=1)
    gh = jnp.dot(u, wgh_ref[...], preferred_element_type=F32) * r
    ga = jnp.dot(u, wga_ref[...], preferred_element_type=F32) * r
    bh = jnp.dot(oh, wbh_ref[...], preferred_element_type=F32)
    ba = jnp.dot(oa_ref[...], wba_ref[...], preferred_element_type=F32)
    o_ref[...] = (_sigmoid(gh) * bh + _sigmoid(ga) * ba).astype(o_ref.dtype)


def _merge(u, r, oh, oa, wgh, wga, wbh, wba):
    m, d = u.shape
    nh = oh.shape[0]
    n = wgh.shape[1]
    tm = _pick(m, ROW_TILES)
    tn = _pick(n, (256, 128))
    return pl.pallas_call(
        _merge_kernel,
        grid=(m // tm, n // tn),
        in_specs=[pl.BlockSpec((tm, d), lambda i, j: (i, 0)),
                  pl.BlockSpec((tm, 1), lambda i, j: (i, 0)),
                  pl.BlockSpec((nh, tm, LANES), lambda i, j: (0, i, 0)),
                  pl.BlockSpec((tm, oa.shape[1]), lambda i, j: (i, 0)),
                  pl.BlockSpec((d, tn), lambda i, j: (0, j)),
                  pl.BlockSpec((d, tn), lambda i, j: (0, j)),
                  pl.BlockSpec((wbh.shape[0], tn), lambda i, j: (0, j)),
                  pl.BlockSpec((wba.shape[0], tn), lambda i, j: (0, j))],
        out_specs=pl.BlockSpec((tm, tn), lambda i, j: (i, j)),
        out_shape=jax.ShapeDtypeStruct((m, n), BF16),
        compiler_params=_cparams(("parallel", "arbitrary")),
        name="branch_merge",
    )(u, r, oh, oa, wgh, wga, wbh, wba)


def _rope_tail(y2, tab):
    w = y2 * tab
    rot = w + pltpu.roll(w, QK_ROPE, axis=1)
    lane = lax.broadcasted_iota(jnp.int32, rot.shape, 1)
    return jnp.where(lane < QK_ROPE, rot, 0.0)


def _rope_sumsq(y2):
    lane = lax.broadcasted_iota(jnp.int32, y2.shape, 1)
    y2m = jnp.where(lane < QK_ROPE, y2, 0.0)
    return jnp.sum(y2m * y2m, axis=-1, keepdims=True)


def _head_rms_scale(y, ones_mask, extra_ss):
    ss = jnp.dot((y * y).astype(BF16), ones_mask, preferred_element_type=F32)
    return lax.rsqrt((ss + extra_ss) * (1.0 / QK_HEAD) + RMS_EPS)


def _latent_norm(x, w):
    ms = jnp.mean(x * x, axis=-1, keepdims=True)
    return ((x * lax.rsqrt(ms + RMS_EPS)) * w).astype(BF16)


def _qkv_kernel(lat_ref, qlw_ref, kvlw_ref, wq_ref, wkv_ref, gq_ref, gk_ref, tab_ref,
                q_ref, k_ref, v_ref, *, q_lora, kv_lora, q_scale):
    xq = _latent_norm(lat_ref[:, 0:q_lora], qlw_ref[...])
    xkv = _latent_norm(lat_ref[:, q_lora:q_lora + kv_lora], kvlw_ref[...])
    kr = lat_ref[:, q_lora + kv_lora:q_lora + kv_lora + LANES]
    gq1 = gq_ref[:, :QK_NOPE] * q_scale
    gk1 = gk_ref[:, :QK_NOPE]
    tab_q = tab_ref[...] * (gq_ref[:, QK_NOPE:] * q_scale)
    tab_k = tab_ref[...] * gk_ref[:, QK_NOPE:]
    kr_ss = _rope_sumsq(kr)
    row = lax.broadcasted_iota(jnp.int32, (QK_PAD, LANES), 0)
    ones_q = jnp.where(row < QK_HEAD, 1.0, 0.0).astype(BF16)
    ones_k = jnp.ones((QK_NOPE, LANES), BF16)

    def body(h, carry):
        y = jnp.dot(xq, wq_ref[h], preferred_element_type=F32)
        y1, y2 = y[:, :QK_NOPE], y[:, QK_NOPE:]
        r = _head_rms_scale(y, ones_q, 0.0)
        q_ref[h, :, 0:QK_NOPE] = ((y1 * r) * gq1).astype(BF16)
        q_ref[h, :, QK_NOPE:] = _rope_tail(y2 * r, tab_q).astype(BF16)

        z = jnp.dot(xkv, wkv_ref[h], preferred_element_type=F32)
        z1 = z[:, :QK_NOPE]
        rk = _head_rms_scale(z1, ones_k, kr_ss)
        k_ref[h, :, 0:QK_NOPE] = ((z1 * rk) * gk1).astype(BF16)
        k_ref[h, :, QK_NOPE:] = _rope_tail(kr * rk, tab_k).astype(BF16)
        v_ref[h] = z[:, QK_NOPE:].astype(BF16)
        return carry

    lax.fori_loop(0, q_ref.shape[0], body, 0, unroll=2)


def _qkv_proj(lat, q_lat_w, kv_lat_w, wq, wkv, gq, gk, tab, row_off, rows, q_lora, kv_lora):
    nh = wq.shape[0]
    tl = _pick(math.gcd(rows, row_off) if row_off else rows, (512, 256, 128, 64, 32, 16))
    off = row_off // tl
    full = lambda shape: pl.BlockSpec(shape, lambda i: (0,) * len(shape))
    return pl.pallas_call(
        functools.partial(_qkv_kernel, q_lora=q_lora, kv_lora=kv_lora,
                          q_scale=LOG2E / math.sqrt(QK_HEAD)),
        grid=(rows // tl,),
        in_specs=[pl.BlockSpec((tl, lat.shape[1]), lambda i: (off + i, 0)),
                  full((1, q_lora)), full((1, kv_lora)),
                  full((nh, q_lora, QK_PAD)), full((nh, kv_lora, QK_NOPE + V_HEAD)),
                  full((1, QK_PAD)), full((1, QK_PAD)),
                  pl.BlockSpec((tl, LANES), lambda i: (i, 0))],
        out_specs=[pl.BlockSpec((nh, tl, QK_PAD), lambda i: (0, i, 0)),
                   pl.BlockSpec((nh, tl, QK_PAD), lambda i: (0, i, 0)),
                   pl.BlockSpec((nh, tl, V_HEAD), lambda i: (0, i, 0))],
        out_shape=[jax.ShapeDtypeStruct((nh, rows, QK_PAD), BF16),
                   jax.ShapeDtypeStruct((nh, rows, QK_PAD), BF16),
                   jax.ShapeDtypeStruct((nh, rows, V_HEAD), BF16)],
        compiler_params=_cparams(("parallel",)),
        name="mla_qkv_proj",
    )(lat, q_lat_w, kv_lat_w, wq, wkv, gq, gk, tab)


def _attn_kernel(q_ref, k_ref, v_ref, o_ref, *, n_valid):
    q = q_ref[0]
    lp = k_ref.shape[1]
    kb = max(QK_PAD, 1 << ((lp // 2).bit_length() - 1))
    edges = list(range(0, min(lp, n_valid), kb)) + [min(lp, -(-n_valid // LANES) * LANES)]
    m = denom = acc = None
    for c0, c1 in zip(edges[:-1], edges[1:]):
        s = lax.dot_general(q, k_ref[0, c0:c1, :], (((1,), (1,)), ((), ())), preferred_element_type=F32)
        if c1 > n_valid:
            col = lax.broadcasted_iota(jnp.int32, s.shape, 1) + c0
            s = jnp.where(col < n_valid, s, NEG_BIG)
        mb = jnp.max(s, axis=-1, keepdims=True)
        if m is None:
            m = mb
            p = jnp.exp2(s - m)
            denom = jnp.sum(p, axis=-1, keepdims=True)
            acc = jnp.dot(p.astype(BF16), v_ref[0, c0:c1, :], preferred_element_type=F32)
        else:
            m_new = jnp.maximum(m, mb)
            alpha = jnp.exp2(m - m_new)
            p = jnp.exp2(s - m_new)
            denom = alpha * denom + jnp.sum(p, axis=-1, keepdims=True)
            acc = alpha * acc + jnp.dot(p.astype(BF16), v_ref[0, c0:c1, :], preferred_element_type=F32)
            m = m_new
    o_ref[...] = (acc / denom).astype(o_ref.dtype)


def _attention(q, k, v, batch, lp, n_valid):
    nh = q.shape[0]
    nq = next(n for n in range(1, lp + 1) if lp % n == 0 and (lp // n) % 16 == 0 and lp // n <= ATTN_MAX_Q_ROWS)
    tq = lp // nq
    return pl.pallas_call(
        functools.partial(_attn_kernel, n_valid=n_valid),
        grid=(batch, nh, nq),
        in_specs=[pl.BlockSpec((1, tq, QK_PAD), lambda b, h, i: (h, b * nq + i, 0)),
                  pl.BlockSpec((1, lp, QK_PAD), lambda b, h, i: (h, b, 0)),
                  pl.BlockSpec((1, lp, V_HEAD), lambda b, h, i: (h, b, 0))],
        out_specs=pl.BlockSpec((tq, V_HEAD), lambda b, h, i: (b * nq + i, h)),
        out_shape=jax.ShapeDtypeStruct((batch * lp, nh * V_HEAD), BF16),
        compiler_params=_cparams(("parallel", "parallel", "arbitrary")),
        name="mla_attention",
    )(q, k, v)


def _lower_bound(logit_ref, h, layer):
    lg = logit_ref[0, h]
    e = jnp.exp(lg - jnp.max(lg, axis=0, keepdims=True))
    p = e / jnp.sum(e, axis=0, keepdims=True)
    lb = jnp.zeros((1, lg.shape[1]), F32)
    for j in range(1, layer + 1):
        lb = lb + p[j:j + 1]
    return lb


def _gates(zq, zf, zi, lb, row_ok):
    q = zq * _sigmoid(zq)
    e = jnp.exp(-jnp.abs(zf))
    r = 1.0 / (1.0 + e)
    er = e * r
    pos = zf >= 0.0
    k = jnp.where(row_ok, (1.0 - lb) * jnp.where(pos, er, r), 0.0)
    f = lb + (1.0 - lb) * jnp.where(pos, r, er)
    g = jnp.maximum(jnp.log(f) * LOG2E, (jnp.log1p(-lb) * LOG2E - 1.0) + jnp.minimum(zf, 0.0) * LOG2E)
    return q, k, zi, g


def _log_cumsum(g, tri):
    g1 = g.astype(BF16)
    r1 = g - g1.astype(F32)
    g2 = r1.astype(BF16)
    g3 = (r1 - g2.astype(F32)).astype(BF16)
    bb = jnp.dot(tri, jnp.concatenate([g1, g2, g3], axis=1), preferred_element_type=F32)
    return (bb[:, :LANES] + bb[:, LANES:2 * LANES]) + bb[:, 2 * LANES:]


def _carried(q, k, v, b, total, st):
    qh = (q * jnp.exp2(b)).astype(BF16)
    kh = (k * jnp.exp2(total - b)).astype(BF16)
    o = lax.dot_general(qh, st.astype(BF16), (((1,), (1,)), ((), ())), preferred_element_type=F32)
    st_new = st * jnp.exp2(total) + lax.dot_general(
        v.astype(BF16), kh, (((0,), (0,)), ((), ())), preferred_element_type=F32)
    return o, st_new


def _hgrn_chunk_factorised(q, k, v, g, st, tri, in_scan, reverse):
    c_rows = q.shape[0]
    sub = HG_SUB_ROWS
    b = _log_cumsum(g, tri)
    total = b[0:1] if reverse else b[c_rows - 1:c_rows]
    o, st_new = _carried(q, k, v, b, total, st)

    zero_row = jnp.zeros((1, LANES), F32)
    growth = zero_row
    rows = []
    for blk in range(c_rows // sub):
        lo, hi = blk * sub, (blk + 1) * sub
        if reverse:
            ref = b[hi:hi + 1] if hi < c_rows else zero_row
            growth = jnp.maximum(growth, ref - b[lo:lo + 1])
        else:
            ref = b[lo - 1:lo] if lo > 0 else zero_row
            growth = jnp.maximum(growth, ref - b[hi - 1:hi])
        qi = (q[lo:hi] * jnp.exp2(b[lo:hi] - ref)).astype(BF16)
        ks = (k * jnp.exp2(jnp.minimum(ref - b, HG_MAX_LOG2_GROWTH))).astype(BF16)
        rows.append(lax.dot_general(qi, ks, (((1,), (1,)), ((), ())), preferred_element_type=F32))
    a = jnp.where(in_scan, jnp.concatenate(rows, axis=0), 0.0)
    o = o + jnp.dot(a.astype(BF16), v.astype(BF16), preferred_element_type=F32)
    return o, st_new, growth


def _hgrn_chunk_exact(q, k, v, g, st, tri, ones_sq, reverse):
    c_rows = q.shape[0]
    sub = HG_EXACT_SUB_ROWS
    nb = c_rows // sub
    b = _log_cumsum(g, tri)
    total = b[0:1] if reverse else b[c_rows - 1:c_rows]
    o, _ = _carried(q, k, v, b, total, st)

    vb = v.astype(BF16)
    pieces = []
    for blk in range(nb):
        lo, hi = blk * sub, (blk + 1) * sub
        if reverse:
            if blk == nb - 1:
                pieces.append(jnp.zeros((sub, LANES), F32))
                continue
            ref = b[hi:hi + 1]
            other = slice(hi, c_rows)
        else:
            if blk == 0:
                pieces.append(jnp.zeros((sub, LANES), F32))
                continue
            ref = b[lo - 1:lo]
            other = slice(0, lo)
        qi = (q[lo:hi] * jnp.exp2(b[lo:hi] - ref)).astype(BF16)
        ko = (k[other] * jnp.exp2(ref - b[other])).astype(BF16)
        a = lax.dot_general(qi, ko, (((1,), (1,)), ((), ())), preferred_element_type=F32)
        pieces.append(jnp.dot(a.astype(BF16), vb[other], preferred_element_type=F32))
    o = o + jnp.concatenate(pieces, axis=0)

    pos = lax.broadcasted_iota(jnp.int32, (c_rows, LANES), 0) % sub
    d_list, v_list = [], []
    for d in range(sub):
        if d == 0:
            vs = v
            dmat = q * k
        else:
            shift = (c_rows - d) if reverse else d
            ks = pltpu.roll(k, shift, axis=0)
            bs = pltpu.roll(b, shift, axis=0)
            vs = pltpu.roll(v, shift, axis=0)
            ok = (pos + d < sub) if reverse else (pos >= d)
            dmat = (q * ks) * jnp.exp2(jnp.where(ok, b - bs, NEG_BIG))
        d_list.append(dmat.astype(BF16))
        v_list.append(vs)
    rs = jnp.dot(jnp.concatenate(d_list, axis=0), ones_sq, preferred_element_type=F32)
    for d in range(sub):
        o = o + rs[d * c_rows:(d + 1) * c_rows] * v_list[d]
    return o


def _hgrn_kernel(info_ref, *refs, layer, reverse):
    if reverse:
        zq_ref, zf_ref, zi_ref, zg_ref, of_ref, lg_ref, nw_ref, o_ref, st_ref = refs
    else:
        zq_ref, zf_ref, zi_ref, lg_ref, o_ref, st_ref = refs
    step = pl.program_id(0)
    g = (pl.num_programs(0) - 1 - step) if reverse else step
    local, n_chunks, seq_rows = info_ref[3 * g], info_ref[3 * g + 1], info_ref[3 * g + 2]
    nh, c_rows = zq_ref.shape[0], zq_ref.shape[1]
    cur = step % 2

    @pl.when((local == n_chunks - 1) if reverse else (local == 0))
    def _():
        st_ref[cur] = jnp.zeros(st_ref.shape[1:], F32)

    rr = lax.broadcasted_iota(jnp.int32, (c_rows, c_rows), 0)
    cc = lax.broadcasted_iota(jnp.int32, (c_rows, c_rows), 1)
    in_scan = (cc >= rr) if reverse else (cc <= rr)
    tri = jnp.where(in_scan, 1.0, 0.0).astype(BF16)
    row_ok = (lax.broadcasted_iota(jnp.int32, (c_rows, LANES), 0) + local * c_rows) < seq_rows

    def finish(h, o):
        if reverse:
            o = o + of_ref[h]
            ms = jnp.mean(o * o, axis=-1, keepdims=True)
            zg = zg_ref[h]
            o_ref[h] = (((o * lax.rsqrt(ms + RMS_EPS)) * nw_ref[...]) * (zg * _sigmoid(zg))).astype(o_ref.dtype)
        else:
            o_ref[h] = o

    def fast_body(h, growth):
        q, k, v, gl = _gates(zq_ref[h], zf_ref[h], zi_ref[h], _lower_bound(lg_ref, h, layer), row_ok)
        o, st_new, gr = _hgrn_chunk_factorised(q, k, v, gl, st_ref[cur, h], tri, in_scan, reverse)
        st_ref[1 - cur, h] = st_new
        finish(h, o)
        return jnp.maximum(growth, gr)

    growth = lax.fori_loop(0, nh, fast_body, jnp.zeros((1, LANES), F32), unroll=16)

    @pl.when(jnp.max(growth) > HG_MAX_LOG2_GROWTH)
    def _():
        ones_sq = jnp.ones((LANES, LANES), BF16)

        def exact_body(h, carry):
            q, k, v, gl = _gates(zq_ref[h], zf_ref[h], zi_ref[h], _lower_bound(lg_ref, h, layer), row_ok)
            finish(h, _hgrn_chunk_exact(q, k, v, gl, st_ref[cur, h], tri, ones_sq, reverse))
            return carry

        lax.fori_loop(0, nh, exact_body, 0)


def _hgrn(hg, info, lb_logits, out_norm, layer, nh):
    m = hg.shape[1]
    c = HG_CHUNK_ROWS
    n_chunks = m // c
    depth = lb_logits.shape[1]
    lg = lb_logits.astype(F32).reshape(2, depth, nh, LANES).transpose(0, 2, 1, 3)

    def blk(grp, rev):
        if rev:
            return pl.BlockSpec((nh, c, LANES), lambda g, info_ref: (grp, n_chunks - 1 - g, 0))
        return pl.BlockSpec((nh, c, LANES), lambda g, info_ref: (grp, g, 0))

    state = pltpu.VMEM((2, nh, LANES, LANES), F32)
    o_f = pl.pallas_call(
        functools.partial(_hgrn_kernel, layer=layer, reverse=False),
        grid_spec=pltpu.PrefetchScalarGridSpec(
            num_scalar_prefetch=1, grid=(n_chunks,),
            in_specs=[blk(0, False), blk(1, False), blk(3, False),
                      pl.BlockSpec((1, nh, depth, LANES), lambda g, info_ref: (0, 0, 0, 0))],
            out_specs=blk(0, False),
            scratch_shapes=[state]),
        out_shape=jax.ShapeDtypeStruct((nh, m, LANES), F32),
        compiler_params=_cparams(("arbitrary",)),
        name="hgrn_forward",
    )(info, hg, hg, hg, lg)

    return pl.pallas_call(
        functools.partial(_hgrn_kernel, layer=layer, reverse=True),
        grid_spec=pltpu.PrefetchScalarGridSpec(
            num_scalar_prefetch=1, grid=(n_chunks,),
            in_specs=[blk(0, True), blk(2, True), blk(3, True), blk(4, True), blk(0, True),
                      pl.BlockSpec((1, nh, depth, LANES), lambda g, info_ref: (1, 0, 0, 0)),
                      pl.BlockSpec((1, LANES), lambda g, info_ref: (0, 0))],
            out_specs=blk(0, True),
            scratch_shapes=[state]),
        out_shape=jax.ShapeDtypeStruct((nh, m, LANES), BF16),
        compiler_params=_cparams(("arbitrary",)),
        name="hgrn_backward",
    )(info, hg, hg, hg, hg, o_f, lg, out_norm.reshape(1, LANES).astype(F32))


def _swap_halves(x, axis):
    a, b = jnp.split(x, 2, axis=axis)
    return jnp.concatenate([b, a], axis=axis)


def _rope_table(rows):
    inv_freq = 1.0 / (ROPE_THETA ** (jnp.arange(0, QK_ROPE, 2, dtype=F32) / QK_ROPE))
    ang = jnp.arange(rows, dtype=F32)[:, None] * inv_freq[None, :]
    c, s = jnp.cos(ang), jnp.sin(ang)
    return jnp.concatenate([c, c, -s, s], axis=1)


def _head_gain(g):
    g = g.astype(F32)
    return jnp.concatenate([g[:QK_NOPE], g[QK_NOPE:], _swap_halves(g[QK_NOPE:], 0)]).reshape(1, QK_PAD)


def _ffn_weights(w_in, w_out):
    d_ff = w_out.shape[0]
    return w_in[:, :d_ff].astype(BF16), w_in[:, d_ff:].astype(BF16), w_out.astype(BF16)


def _ffn(h, hb, r, weights, next_gain, scale=0.5):
    wg, wu, wo = weights
    return _matmul_residual(_ffn_in(hb, r, wg, wu), wo, h, scale, next_gain)


def kernel(x_prompt, x_sample, meta_tokens, hgrn_lb_logits, norm_ffn1, w_ffn1_in, w_ffn1_out, norm_mix, w_in,
           q_lat_norm, w_uq, kv_lat_norm, w_ukv, q_head_norm, k_head_norm, hg_out_norm,
           w_branch_hgrn, w_branch_mla, w_out, norm_ffn2, w_ffn2_in, w_ffn2_out):
    depth = norm_ffn1.shape[0]
    d_model = x_prompt.shape[-1]
    hg_dim = hgrn_lb_logits.shape[-1]
    hg_heads = hg_dim // HG_HEAD_DIM
    q_lora = q_lat_norm.shape[-1]
    kv_lora = kv_lat_norm.shape[-1]
    mla_heads = w_uq.shape[-1] // QK_HEAD

    trunks = []
    row_off = 0
    blocks, info = [], []
    meta = meta_tokens.astype(F32)
    pad = jnp.zeros((SEQ_PAD - N_META, d_model), F32)
    for x in (x_prompt, x_sample):
        bsz, s, _ = x.shape
        lp = s + SEQ_PAD
        for b in range(bsz):
            blocks += [meta, x[b].astype(F32), pad]
        nc = lp // HG_CHUNK_ROWS
        for _ in range(bsz):
            for c in range(nc):
                info += [c, nc, s + N_META]
        trunks.append((bsz, s, lp, row_off))
        row_off += bsz * lp
    h = jnp.concatenate(blocks, axis=0)
    hb, r = _row_scale(h, norm_ffn1[0].reshape(1, d_model).astype(F32))
    info = jnp.asarray(info, jnp.int32)
    tabs = [jnp.tile(_rope_table(lp), (bsz, 1)) for (bsz, s, lp, _) in trunks]

    sizes = (hg_dim,) * 5 + (q_lora, kv_lora, QK_ROPE, d_model, d_model)
    offs = [0]
    for sz in sizes:
        offs.append(offs[-1] + sz)

    for l in range(depth):
        ffn1 = _ffn_weights(w_ffn1_in[l], w_ffn1_out[l])
        ffn2 = _ffn_weights(w_ffn2_in[l], w_ffn2_out[l])
        wl = w_in[l]
        w_hg = wl[:, :offs[5]].astype(BF16)
        w_kr = wl[:, offs[7]:offs[8]]
        w_lat = jnp.concatenate([wl[:, offs[5]:offs[7]], w_kr, _swap_halves(w_kr, 1)], axis=1).astype(BF16)
        w_gh = wl[:, offs[8]:offs[9]].astype(BF16)
        w_ga = wl[:, offs[9]:offs[10]].astype(BF16)
        after_ffn2 = norm_ffn1[l + 1] if l + 1 < depth else jnp.ones((d_model,), F32)
        wq = w_uq[l].reshape(q_lora, mla_heads, QK_HEAD)
        wq = jnp.concatenate([wq, _swap_halves(wq[:, :, QK_NOPE:], 2)], axis=2)
        wq = wq.transpose(1, 0, 2).astype(BF16)
        wkv = w_ukv[l].reshape(kv_lora, mla_heads, QK_NOPE + V_HEAD).transpose(1, 0, 2).astype(BF16)

        h, hb, r = _ffn(h, hb, r, ffn1, norm_mix[l])

        hg = _matmul(hb, r, w_hg, head_major=True)
        lat = _matmul(hb, r, w_lat)
        o_h = _hgrn(hg, info, hgrn_lb_logits, hg_out_norm[l], l, hg_heads)

        o_a = []
        for (bsz, s, lp, off), tab in zip(trunks, tabs):
            q, k, v = _qkv_proj(lat, q_lat_norm[l].reshape(1, q_lora).astype(F32),
                                kv_lat_norm[l].reshape(1, kv_lora).astype(F32), wq, wkv,
                                _head_gain(q_head_norm[l]), _head_gain(k_head_norm[l]),
                                tab, off, bsz * lp, q_lora, kv_lora)
            o_a.append(_attention(q, k, v, bsz, lp, s + N_META))
        o_a = jnp.concatenate(o_a, axis=0)

        merged = _merge(hb, r, o_h, o_a, w_gh, w_ga, w_branch_hgrn[l].astype(BF16), w_branch_mla[l].astype(BF16))
        h, hb, r = _matmul_residual(merged, w_out[l].astype(BF16), h, 1.0, norm_ffn2[l])

        h, hb, r = _ffn(h, hb, r, ffn2, after_ffn2)

    outs = []
    for (bsz, s, lp, off) in trunks:
        outs.append(h[off:off + bsz * lp].reshape(bsz, lp, d_model)[:, N_META:N_META + s])
    return tuple(outs)
```

```python
import functools
import math

import jax
import jax.numpy as jnp
from jax import lax
from jax.experimental import pallas as pl
from jax.experimental.pallas import tpu as pltpu

N_META = 16
HG_HEAD_DIM = 128
QK_NOPE = 128
QK_ROPE = 64
QK_HEAD = QK_NOPE + QK_ROPE
V_HEAD = 128
ROPE_THETA = 10000.0
RMS_EPS = 1e-6

LANES = 128
SEQ_PAD = 128
HG_CHUNK_ROWS = 128
ROW_TILES = (1024, 512, 256, 128, 64, 32, 16)
HG_SUB_ROWS = 32
HG_EXACT_SUB_ROWS = 16
HG_MAX_LOG2_GROWTH = 115.0
ATTN_MAX_Q_ROWS = 640
QK_PAD = 256
RES_LONG_K = 8192
RES_LONG_K_TILE = 22 * LANES
VMEM_LIMIT = 56 * 1024 * 1024
VMEM_LIMIT_LARGE = 58 * 1024 * 1024
F32 = jnp.float32
BF16 = jnp.bfloat16
NEG_BIG = -1e30
LN2 = math.log(2.0)
LOG2E = 1.0 / LN2


def _cparams(sem, vmem_limit=VMEM_LIMIT):
    return pltpu.CompilerParams(dimension_semantics=sem, vmem_limit_bytes=vmem_limit)


def _pick(n, prefs):
    for p in prefs:
        if n % p == 0:
            return p
    return n


def _sigmoid(x):
    return 1.0 / (1.0 + jnp.exp(-x))


def _row_scale_kernel(x_ref, w_ref, hb_ref, r_ref):
    x = x_ref[...]
    hb_ref[...] = (x * w_ref[...]).astype(BF16)
    r_ref[...] = lax.rsqrt(jnp.mean(x * x, axis=-1, keepdims=True) + RMS_EPS)


def _row_scale(x, w):
    m, d = x.shape
    tm = _pick(m, (512, 256, 128, 64, 32, 16, 8))
    return pl.pallas_call(
        _row_scale_kernel,
        grid=(m // tm,),
        in_specs=[pl.BlockSpec((tm, d), lambda i: (i, 0)),
                  pl.BlockSpec((1, d), lambda i: (0, 0))],
        out_specs=[pl.BlockSpec((tm, d), lambda i: (i, 0)),
                   pl.BlockSpec((tm, 1), lambda i: (i, 0))],
        out_shape=[jax.ShapeDtypeStruct((m, d), BF16), jax.ShapeDtypeStruct((m, 1), F32)],
        compiler_params=_cparams(("parallel",)),
        name="row_scale",
    )(x, w)


def _mm_kernel(a_ref, r_ref, b_ref, o_ref):
    o_ref[...] = jnp.dot(a_ref[...], b_ref[...], preferred_element_type=F32) * r_ref[...]


def _mm_headmajor_kernel(a_ref, r_ref, b_ref, o_ref):
    acc = jnp.dot(a_ref[...], b_ref[...], preferred_element_type=F32) * r_ref[...]
    for j in range(o_ref.shape[0]):
        o_ref[j] = acc[:, j * LANES:(j + 1) * LANES]


def _matmul(a, r, b, *, head_major=False):
    m, k = a.shape
    n = b.shape[1]
    if n % 256 == 0:
        tm = _pick(m, ROW_TILES)
        tn = _pick(n, (1024, 512, 256))
    else:
        tm = _pick(m, (512, 256, 128, 64, 32, 16))
        tn = n
    grid = (m // tm, n // tn)
    in_specs = [pl.BlockSpec((tm, k), lambda i, j: (i, 0)),
                pl.BlockSpec((tm, 1), lambda i, j: (i, 0)),
                pl.BlockSpec((k, tn), lambda i, j: (0, j))]
    if head_major:
        return pl.pallas_call(
            _mm_headmajor_kernel, grid=grid, in_specs=in_specs,
            out_specs=pl.BlockSpec((tn // LANES, tm, LANES), lambda i, j: (j, i, 0)),
            out_shape=jax.ShapeDtypeStruct((n // LANES, m, LANES), F32),
            compiler_params=_cparams(("parallel", "arbitrary")),
            name="proj_headmajor",
        )(a, r, b)
    return pl.pallas_call(
        _mm_kernel, grid=grid, in_specs=in_specs,
        out_specs=pl.BlockSpec((tm, tn), lambda i, j: (i, j)),
        out_shape=jax.ShapeDtypeStruct((m, n), F32),
        compiler_params=_cparams(("parallel", "arbitrary")),
        name="proj",
    )(a, r, b)


def _ffn_in_kernel(x_ref, r_ref, wg_ref, wu_ref, o_ref):
    x = x_ref[...]
    r = r_ref[...]
    g = jnp.dot(x, wg_ref[...], preferred_element_type=F32) * r
    u = jnp.dot(x, wu_ref[...], preferred_element_type=F32) * r
    o_ref[...] = ((g * _sigmoid(g)) * u).astype(o_ref.dtype)


def _ffn_in(x, r, wg, wu):
    m, k = x.shape
    f = wg.shape[1]
    tm = _pick(m, ROW_TILES)
    tf = min(512, f)
    return pl.pallas_call(
        _ffn_in_kernel,
        grid=(m // tm, pl.cdiv(f, tf)),
        in_specs=[pl.BlockSpec((tm, k), lambda i, j: (i, 0)),
                  pl.BlockSpec((tm, 1), lambda i, j: (i, 0)),
                  pl.BlockSpec((k, tf), lambda i, j: (0, j)),
                  pl.BlockSpec((k, tf), lambda i, j: (0, j))],
        out_specs=pl.BlockSpec((tm, tf), lambda i, j: (i, j)),
        out_shape=jax.ShapeDtypeStruct((m, f), BF16),
        compiler_params=_cparams(("parallel", "arbitrary")),
        name="ffn_in",
    )(x, r, wg, wu)


def _mm_res_kernel(a_ref, b_ref, res_ref, w_ref, o_ref, hb_ref, rs_ref, ss_ref, *, scale, k_tail):
    j, kk = pl.program_id(1), pl.program_id(2)

    @pl.when(kk == 0)
    def _():
        o_ref[...] = res_ref[...]

    last = pl.num_programs(2) - 1

    @pl.when(kk < last)
    def _():
        o_ref[...] += scale * jnp.dot(a_ref[...], b_ref[...], preferred_element_type=F32)

    @pl.when(kk == last)
    def _():
        a, b = a_ref[...], b_ref[...]
        if k_tail:
            a = jnp.where(lax.broadcasted_iota(jnp.int32, a.shape, 1) < k_tail, a, jnp.zeros_like(a))
            b = jnp.where(lax.broadcasted_iota(jnp.int32, b.shape, 0) < k_tail, b, jnp.zeros_like(b))
        o = o_ref[...] + scale * jnp.dot(a, b, preferred_element_type=F32)
        o_ref[...] = o
        hb_ref[...] = (o * w_ref[...]).astype(BF16)
        part = jnp.sum(o * o, axis=-1, keepdims=True)
        ss = jnp.where(j == 0, part, ss_ref[...] + part)
        ss_ref[...] = ss
        d_model = o.shape[1] * pl.num_programs(1)
        rs_ref[...] = lax.rsqrt(ss * (1.0 / d_model) + RMS_EPS)


def _matmul_residual(a, b, res, scale, next_gain):
    m, k = a.shape
    n = b.shape[1]
    tm = _pick(m, ROW_TILES)
    if k > RES_LONG_K:
        tn, tk = _pick(n, (1024, 512, 256, 128)), RES_LONG_K_TILE
    else:
        tn, tk = _pick(n, (2048, 1024, 512, 256, 128)), min(1024, k)
    return pl.pallas_call(
        functools.partial(_mm_res_kernel, scale=scale, k_tail=k % tk),
        grid=(m // tm, n // tn, pl.cdiv(k, tk)),
        in_specs=[pl.BlockSpec((tm, tk), lambda i, j, kk: (i, kk)),
                  pl.BlockSpec((tk, tn), lambda i, j, kk: (kk, j)),
                  pl.BlockSpec((tm, tn), lambda i, j, kk: (i, j)),
                  pl.BlockSpec((1, tn), lambda i, j, kk: (0, j))],
        out_specs=[pl.BlockSpec((tm, tn), lambda i, j, kk: (i, j)),
                   pl.BlockSpec((tm, tn), lambda i, j, kk: (i, j)),
                   pl.BlockSpec((tm, 1), lambda i, j, kk: (i, 0))],
        out_shape=[jax.ShapeDtypeStruct((m, n), F32), jax.ShapeDtypeStruct((m, n), BF16),
                   jax.ShapeDtypeStruct((m, 1), F32)],
        scratch_shapes=[pltpu.VMEM((tm, 1), F32)],
        compiler_params=_cparams(("parallel", "arbitrary", "arbitrary"), VMEM_LIMIT_LARGE),
        name="proj_residual",
    )(a, b, res, next_gain.reshape(1, n).astype(F32))


def _merge_kernel(u_ref, r_ref, oh_ref, oa_ref, wgh_ref, wga_ref, wbh_ref, wba_ref, o_ref):
    u = u_ref[...]
    r = r_ref[...]
    oh = jnp.concatenate([oh_ref[h] for h in range(oh_ref.shape[0])], axis=1)
    gh = jnp.dot(u, wgh_ref[...], preferred_element_type=F32) * r
    ga = jnp.dot(u, wga_ref[...], preferred_element_type=F32) * r
    bh = jnp.dot(oh, wbh_ref[...], preferred_element_type=F32)
    ba = jnp.dot(oa_ref[...], wba_ref[...], preferred_element_type=F32)
    o_ref[...] = (_sigmoid(gh) * bh + _sigmoid(ga) * ba).astype(o_ref.dtype)


def _merge(u, r, oh, oa, wgh, wga, wbh, wba):
    m, d = u.shape
    nh = oh.shape[0]
    n = wgh.shape[1]
    tm = _pick(m, ROW_TILES)
    tn = _pick(n, (256, 128))
    return pl.pallas_call(
        _merge_kernel,
        grid=(m // tm, n // tn),
        in_specs=[pl.BlockSpec((tm, d), lambda i, j: (i, 0)),
                  pl.BlockSpec((tm, 1), lambda i, j: (i, 0)),
                  pl.BlockSpec((nh, tm, LANES), lambda i, j: (0, i, 0)),
                  pl.BlockSpec((tm, oa.shape[1]), lambda i, j: (i, 0)),
                  pl.BlockSpec((d, tn), lambda i, j: (0, j)),
                  pl.BlockSpec((d, tn), lambda i, j: (0, j)),
                  pl.BlockSpec((wbh.shape[0], tn), lambda i, j: (0, j)),
                  pl.BlockSpec((wba.shape[0], tn), lambda i, j: (0, j))],
        out_specs=pl.BlockSpec((tm, tn), lambda i, j: (i, j)),
        out_shape=jax.ShapeDtypeStruct((m, n), BF16),
        compiler_params=_cparams(("parallel", "arbitrary")),
        name="branch_merge",
    )(u, r, oh, oa, wgh, wga, wbh, wba)


def _rope_tail(y2, tab):
    w = y2 * tab
    rot = w + pltpu.roll(w, QK_ROPE, axis=1)
    lane = lax.broadcasted_iota(jnp.int32, rot.shape, 1)
    return jnp.where(lane < QK_ROPE, rot, 0.0)


def _rope_sumsq(y2):
    lane = lax.broadcasted_iota(jnp.int32, y2.shape, 1)
    y2m = jnp.where(lane < QK_ROPE, y2, 0.0)
    return jnp.sum(y2m * y2m, axis=-1, keepdims=True)


def _head_rms_scale(y, ones_mask, extra_ss):
    ss = jnp.dot((y * y).astype(BF16), ones_mask, preferred_element_type=F32)
    return lax.rsqrt((ss + extra_ss) * (1.0 / QK_HEAD) + RMS_EPS)


def _latent_norm(x, w):
    ms = jnp.mean(x * x, axis=-1, keepdims=True)
    return ((x * lax.rsqrt(ms + RMS_EPS)) * w).astype(BF16)


def _qkv_kernel(lat_ref, qlw_ref, kvlw_ref, wq_ref, wkv_ref, gq_ref, gk_ref, tab_ref,
                q_ref, k_ref, v_ref, *, q_lora, kv_lora, q_scale):
    xq = _latent_norm(lat_ref[:, 0:q_lora], qlw_ref[...])
    xkv = _latent_norm(lat_ref[:, q_lora:q_lora + kv_lora], kvlw_ref[...])
    kr = lat_ref[:, q_lora + kv_lora:q_lora + kv_lora + LANES]
    gq1 = gq_ref[:, :QK_NOPE] * q_scale
    gk1 = gk_ref[:, :QK_NOPE]
    tab_q = tab_ref[...] * (gq_ref[:, QK_NOPE:] * q_scale)
    tab_k = tab_ref[...] * gk_ref[:, QK_NOPE:]
    kr_ss = _rope_sumsq(kr)
    row = lax.broadcasted_iota(jnp.int32, (QK_PAD, LANES), 0)
    ones_q = jnp.where(row < QK_HEAD, 1.0, 0.0).astype(BF16)
    ones_k = jnp.ones((QK_NOPE, LANES), BF16)

    def body(h, carry):
        y = jnp.dot(xq, wq_ref[h], preferred_element_type=F32)
        y1, y2 = y[:, :QK_NOPE], y[:, QK_NOPE:]
        r = _head_rms_scale(y, ones_q, 0.0)
        q_ref[h, :, 0:QK_NOPE] = ((y1 * r) * gq1).astype(BF16)
        q_ref[h, :, QK_NOPE:] = _rope_tail(y2 * r, tab_q).astype(BF16)

        z = jnp.dot(xkv, wkv_ref[h], preferred_element_type=F32)
        z1 = z[:, :QK_NOPE]
        rk = _head_rms_scale(z1, ones_k, kr_ss)
        k_ref[h, :, 0:QK_NOPE] = ((z1 * rk) * gk1).astype(BF16)
        k_ref[h, :, QK_NOPE:] = _rope_tail(kr * rk, tab_k).astype(BF16)
        v_ref[h] = z[:, QK_NOPE:].astype(BF16)
        return carry

    lax.fori_loop(0, q_ref.shape[0], body, 0, unroll=2)


def _qkv_proj(lat, q_lat_w, kv_lat_w, wq, wkv, gq, gk, tab, row_off, rows, q_lora, kv_lora):
    nh = wq.shape[0]
    tl = _pick(math.gcd(rows, row_off) if row_off else rows, (512, 256, 128, 64, 32, 16))
    off = row_off // tl
    full = lambda shape: pl.BlockSpec(shape, lambda i: (0,) * len(shape))
    return pl.pallas_call(
        functools.partial(_qkv_kernel, q_lora=q_lora, kv_lora=kv_lora,
                          q_scale=LOG2E / math.sqrt(QK_HEAD)),
        grid=(rows // tl,),
        in_specs=[pl.BlockSpec((tl, lat.shape[1]), lambda i: (off + i, 0)),
                  full((1, q_lora)), full((1, kv_lora)),
                  full((nh, q_lora, QK_PAD)), full((nh, kv_lora, QK_NOPE + V_HEAD)),
                  full((1, QK_PAD)), full((1, QK_PAD)),
                  pl.BlockSpec((tl, LANES), lambda i: (i, 0))],
        out_specs=[pl.BlockSpec((nh, tl, QK_PAD), lambda i: (0, i, 0)),
                   pl.BlockSpec((nh, tl, QK_PAD), lambda i: (0, i, 0)),
                   pl.BlockSpec((nh, tl, V_HEAD), lambda i: (0, i, 0))],
        out_shape=[jax.ShapeDtypeStruct((nh, rows, QK_PAD), BF16),
                   jax.ShapeDtypeStruct((nh, rows, QK_PAD), BF16),
                   jax.ShapeDtypeStruct((nh, rows, V_HEAD), BF16)],
        compiler_params=_cparams(("parallel",)),
        name="mla_qkv_proj",
    )(lat, q_lat_w, kv_lat_w, wq, wkv, gq, gk, tab)


def _attn_kernel(q_ref, k_ref, v_ref, o_ref, *, n_valid):
    q = q_ref[0]
    lp = k_ref.shape[1]
    kb = max(QK_PAD, 1 << ((lp // 2).bit_length() - 1))
    edges = list(range(0, min(lp, n_valid), kb)) + [min(lp, -(-n_valid // LANES) * LANES)]
    m = denom = acc = None
    for c0, c1 in zip(edges[:-1], edges[1:]):
        s = lax.dot_general(q, k_ref[0, c0:c1, :], (((1,), (1,)), ((), ())), preferred_element_type=F32)
        if c1 > n_valid:
            col = lax.broadcasted_iota(jnp.int32, s.shape, 1) + c0
            s = jnp.where(col < n_valid, s, NEG_BIG)
        mb = jnp.max(s, axis=-1, keepdims=True)
        if m is None:
            m = mb
            p = jnp.exp2(s - m)
            denom = jnp.sum(p, axis=-1, keepdims=True)
            acc = jnp.dot(p.astype(BF16), v_ref[0, c0:c1, :], preferred_element_type=F32)
        else:
            m_new = jnp.maximum(m, mb)
            alpha = jnp.exp2(m - m_new)
            p = jnp.exp2(s - m_new)
            denom = alpha * denom + jnp.sum(p, axis=-1, keepdims=True)
            acc = alpha * acc + jnp.dot(p.astype(BF16), v_ref[0, c0:c1, :], preferred_element_type=F32)
            m = m_new
    o_ref[...] = (acc / denom).astype(o_ref.dtype)


def _attention(q, k, v, batch, lp, n_valid):
    nh = q.shape[0]
    nq = next(n for n in range(1, lp + 1) if lp % n == 0 and (lp // n) % 16 == 0 and lp // n <= ATTN_MAX_Q_ROWS)
    tq = lp // nq
    return pl.pallas_call(
        functools.partial(_attn_kernel, n_valid=n_valid),
        grid=(batch, nh, nq),
        in_specs=[pl.BlockSpec((1, tq, QK_PAD), lambda b, h, i: (h, b * nq + i, 0)),
                  pl.BlockSpec((1, lp, QK_PAD), lambda b, h, i: (h, b, 0)),
                  pl.BlockSpec((1, lp, V_HEAD), lambda b, h, i: (h, b, 0))],
        out_specs=pl.BlockSpec((tq, V_HEAD), lambda b, h, i: (b * nq + i, h)),
        out_shape=jax.ShapeDtypeStruct((batch * lp, nh * V_HEAD), BF16),
        compiler_params=_cparams(("parallel", "parallel", "arbitrary")),
        name="mla_attention",
    )(q, k, v)


def _lower_bound(logit_ref, h, layer):
    lg = logit_ref[0, h]
    e = jnp.exp(lg - jnp.max(lg, axis=0, keepdims=True))
    p = e / jnp.sum(e, axis=0, keepdims=True)
    lb = jnp.zeros((1, lg.shape[1]), F32)
    for j in range(1, layer + 1):
        lb = lb + p[j:j + 1]
    return lb


def _gates(zq, zf, zi, lb, row_ok):
    q = zq * _sigmoid(zq)
    e = jnp.exp(-jnp.abs(zf))
    r = 1.0 / (1.0 + e)
    er = e * r
    pos = zf >= 0.0
    k = jnp.where(row_ok, (1.0 - lb) * jnp.where(pos, er, r), 0.0)
    f = lb + (1.0 - lb) * jnp.where(pos, r, er)
    g = jnp.maximum(jnp.log(f) * LOG2E, (jnp.log1p(-lb) * LOG2E - 1.0) + jnp.minimum(zf, 0.0) * LOG2E)
    return q, k, zi, g


def _log_cumsum(g, tri):
    g1 = g.astype(BF16)
    r1 = g - g1.astype(F32)
    g2 = r1.astype(BF16)
    g3 = (r1 - g2.astype(F32)).astype(BF16)
    bb = jnp.dot(tri, jnp.concatenate([g1, g2, g3], axis=1), preferred_element_type=F32)
    return (bb[:, :LANES] + bb[:, LANES:2 * LANES]) + bb[:, 2 * LANES:]


def _carried(q, k, v, b, total, st):
    qh = (q * jnp.exp2(b)).astype(BF16)
    kh = (k * jnp.exp2(total - b)).astype(BF16)
    o = lax.dot_general(qh, st.astype(BF16), (((1,), (1,)), ((), ())), preferred_element_type=F32)
    st_new = st * jnp.exp2(total) + lax.dot_general(
        v.astype(BF16), kh, (((0,), (0,)), ((), ())), preferred_element_type=F32)
    return o, st_new


def _hgrn_chunk_factorised(q, k, v, g, st, tri, in_scan, reverse):
    c_rows = q.shape[0]
    sub = HG_SUB_ROWS
    b = _log_cumsum(g, tri)
    total = b[0:1] if reverse else b[c_rows - 1:c_rows]
    o, st_new = _carried(q, k, v, b, total, st)

    zero_row = jnp.zeros((1, LANES), F32)
    growth = zero_row
    rows = []
    for blk in range(c_rows // sub):
        lo, hi = blk * sub, (blk + 1) * sub
        if reverse:
            ref = b[hi:hi + 1] if hi < c_rows else zero_row
            growth = jnp.maximum(growth, ref - b[lo:lo + 1])
        else:
            ref = b[lo - 1:lo] if lo > 0 else zero_row
            growth = jnp.maximum(growth, ref - b[hi - 1:hi])
        qi = (q[lo:hi] * jnp.exp2(b[lo:hi] - ref)).astype(BF16)
        ks = (k * jnp.exp2(jnp.minimum(ref - b, HG_MAX_LOG2_GROWTH))).astype(BF16)
        rows.append(lax.dot_general(qi, ks, (((1,), (1,)), ((), ())), preferred_element_type=F32))
    a = jnp.where(in_scan, jnp.concatenate(rows, axis=0), 0.0)
    o = o + jnp.dot(a.astype(BF16), v.astype(BF16), preferred_element_type=F32)
    return o, st_new, growth


def _hgrn_chunk_exact(q, k, v, g, st, tri, ones_sq, reverse):
    c_rows = q.shape[0]
    sub = HG_EXACT_SUB_ROWS
    nb = c_rows // sub
    b = _log_cumsum(g, tri)
    total = b[0:1] if reverse else b[c_rows - 1:c_rows]
    o, _ = _carried(q, k, v, b, total, st)

    vb = v.astype(BF16)
    pieces = []
    for blk in range(nb):
        lo, hi = blk * sub, (blk + 1) * sub
        if reverse:
            if blk == nb - 1:
                pieces.append(jnp.zeros((sub, LANES), F32))
                continue
            ref = b[hi:hi + 1]
            other = slice(hi, c_rows)
        else:
            if blk == 0:
                pieces.append(jnp.zeros((sub, LANES), F32))
                continue
            ref = b[lo - 1:lo]
            other = slice(0, lo)
        qi = (q[lo:hi] * jnp.exp2(b[lo:hi] - ref)).astype(BF16)
        ko = (k[other] * jnp.exp2(ref - b[other])).astype(BF16)
        a = lax.dot_general(qi, ko, (((1,), (1,)), ((), ())), preferred_element_type=F32)
        pieces.append(jnp.dot(a.astype(BF16), vb[other], preferred_element_type=F32))
    o = o + jnp.concatenate(pieces, axis=0)

    pos = lax.broadcasted_iota(jnp.int32, (c_rows, LANES), 0) % sub
    d_list, v_list = [], []
    for d in range(sub):
        if d == 0:
            vs = v
            dmat = q * k
        else:
            shift = (c_rows - d) if reverse else d
            ks = pltpu.roll(k, shift, axis=0)
            bs = pltpu.roll(b, shift, axis=0)
            vs = pltpu.roll(v, shift, axis=0)
            ok = (pos + d < sub) if reverse else (pos >= d)
            dmat = (q * ks) * jnp.exp2(jnp.where(ok, b - bs, NEG_BIG))
        d_list.append(dmat.astype(BF16))
        v_list.append(vs)
    rs = jnp.dot(jnp.concatenate(d_list, axis=0), ones_sq, preferred_element_type=F32)
    for d in range(sub):
        o = o + rs[d * c_rows:(d + 1) * c_rows] * v_list[d]
    return o


def _hgrn_kernel(info_ref, *refs, layer, reverse):
    if reverse:
        zq_ref, zf_ref, zi_ref, zg_ref, of_ref, lg_ref, nw_ref, o_ref, st_ref = refs
    else:
        zq_ref, zf_ref, zi_ref, lg_ref, o_ref, st_ref = refs
    step = pl.program_id(0)
    g = (pl.num_programs(0) - 1 - step) if reverse else step
    local, n_chunks, seq_rows = info_ref[3 * g], info_ref[3 * g + 1], info_ref[3 * g + 2]
    nh, c_rows = zq_ref.shape[0], zq_ref.shape[1]
    cur = step % 2

    @pl.when((local == n_chunks - 1) if reverse else (local == 0))
    def _():
        st_ref[cur] = jnp.zeros(st_ref.shape[1:], F32)

    rr = lax.broadcasted_iota(jnp.int32, (c_rows, c_rows), 0)
    cc = lax.broadcasted_iota(jnp.int32, (c_rows, c_rows), 1)
    in_scan = (cc >= rr) if reverse else (cc <= rr)
    tri = jnp.where(in_scan, 1.0, 0.0).astype(BF16)
    row_ok = (lax.broadcasted_iota(jnp.int32, (c_rows, LANES), 0) + local * c_rows) < seq_rows

    def finish(h, o):
        if reverse:
            o = o + of_ref[h]
            ms = jnp.mean(o * o, axis=-1, keepdims=True)
            zg = zg_ref[h]
            o_ref[h] = (((o * lax.rsqrt(ms + RMS_EPS)) * nw_ref[...]) * (zg * _sigmoid(zg))).astype(o_ref.dtype)
        else:
            o_ref[h] = o

    def fast_body(h, growth):
        q, k, v, gl = _gates(zq_ref[h], zf_ref[h], zi_ref[h], _lower_bound(lg_ref, h, layer), row_ok)
        o, st_new, gr = _hgrn_chunk_factorised(q, k, v, gl, st_ref[cur, h], tri, in_scan, reverse)
        st_ref[1 - cur, h] = st_new
        finish(h, o)
        return jnp.maximum(growth, gr)

    growth = lax.fori_loop(0, nh, fast_body, jnp.zeros((1, LANES), F32), unroll=16)

    @pl.when(jnp.max(growth) > HG_MAX_LOG2_GROWTH)
    def _():
        ones_sq = jnp.ones((LANES, LANES), BF16)

        def exact_body(h, carry):
            q, k, v, gl = _gates(zq_ref[h], zf_ref[h], zi_ref[h], _lower_bound(lg_ref, h, layer), row_ok)
            finish(h, _hgrn_chunk_exact(q, k, v, gl, st_ref[cur, h], tri, ones_sq, reverse))
            return carry

        lax.fori_loop(0, nh, exact_body, 0)


def _hgrn(hg, info, lb_logits, out_norm, layer, nh):
    m = hg.shape[1]
    c = HG_CHUNK_ROWS
    n_chunks = m // c
    depth = lb_logits.shape[1]
    lg = lb_logits.astype(F32).reshape(2, depth, nh, LANES).transpose(0, 2, 1, 3)

    def blk(grp, rev):
        if rev:
            return pl.BlockSpec((nh, c, LANES), lambda g, info_ref: (grp, n_chunks - 1 - g, 0))
        return pl.BlockSpec((nh, c, LANES), lambda g, info_ref: (grp, g, 0))

    state = pltpu.VMEM((2, nh, LANES, LANES), F32)
    o_f = pl.pallas_call(
        functools.partial(_hgrn_kernel, layer=layer, reverse=False),
        grid_spec=pltpu.PrefetchScalarGridSpec(
            num_scalar_prefetch=1, grid=(n_chunks,),
            in_specs=[blk(0, False), blk(1, False), blk(3, False),
                      pl.BlockSpec((1, nh, depth, LANES), lambda g, info_ref: (0, 0, 0, 0))],
            out_specs=blk(0, False),
            scratch_shapes=[state]),
        out_shape=jax.ShapeDtypeStruct((nh, m, LANES), F32),
        compiler_params=_cparams(("arbitrary",)),
        name="hgrn_forward",
    )(info, hg, hg, hg, lg)

    return pl.pallas_call(
        functools.partial(_hgrn_kernel, layer=layer, reverse=True),
        grid_spec=pltpu.PrefetchScalarGridSpec(
            num_scalar_prefetch=1, grid=(n_chunks,),
            in_specs=[blk(0, True), blk(2, True), blk(3, True), blk(4, True), blk(0, True),
                      pl.BlockSpec((1, nh, depth, LANES), lambda g, info_ref: (1, 0, 0, 0)),
                      pl.BlockSpec((1, LANES), lambda g, info_ref: (0, 0))],
            out_specs=blk(0, True),
            scratch_shapes=[state]),
        out_shape=jax.ShapeDtypeStruct((nh, m, LANES), BF16),
        compiler_params=_cparams(("arbitrary",)),
        name="hgrn_backward",
    )(info, hg, hg, hg, hg, o_f, lg, out_norm.reshape(1, LANES).astype(F32))


def _swap_halves(x, axis):
    a, b = jnp.split(x, 2, axis=axis)
    return jnp.concatenate([b, a], axis=axis)


def _rope_table(rows):
    inv_freq = 1.0 / (ROPE_THETA ** (jnp.arange(0, QK_ROPE, 2, dtype=F32) / QK_ROPE))
    ang = jnp.arange(rows, dtype=F32)[:, None] * inv_freq[None, :]
    c, s = jnp.cos(ang), jnp.sin(ang)
    return jnp.concatenate([c, c, -s, s], axis=1)


def _head_gain(g):
    g = g.astype(F32)
    return jnp.concatenate([g[:QK_NOPE], g[QK_NOPE:], _swap_halves(g[QK_NOPE:], 0)]).reshape(1, QK_PAD)


def _ffn_weights(w_in, w_out):
    d_ff = w_out.shape[0]
    return w_in[:, :d_ff].astype(BF16), w_in[:, d_ff:].astype(BF16), w_out.astype(BF16)


def _ffn(h, hb, r, weights, next_gain, scale=0.5):
    wg, wu, wo = weights
    return _matmul_residual(_ffn_in(hb, r, wg, wu), wo, h, scale, next_gain)


def kernel(x_prompt, x_sample, meta_tokens, hgrn_lb_logits, norm_ffn1, w_ffn1_in, w_ffn1_out, norm_mix, w_in,
           q_lat_norm, w_uq, kv_lat_norm, w_ukv, q_head_norm, k_head_norm, hg_out_norm,
           w_branch_hgrn, w_branch_mla, w_out, norm_ffn2, w_ffn2_in, w_ffn2_out):
    depth = norm_ffn1.shape[0]
    d_model = x_prompt.shape[-1]
    hg_dim = hgrn_lb_logits.shape[-1]
    hg_heads = hg_dim // HG_HEAD_DIM
    q_lora = q_lat_norm.shape[-1]
    kv_lora = kv_lat_norm.shape[-1]
    mla_heads = w_uq.shape[-1] // QK_HEAD

    trunks = []
    row_off = 0
    blocks, info = [], []
    meta = meta_tokens.astype(F32)
    pad = jnp.zeros((SEQ_PAD - N_META, d_model), F32)
    for x in (x_prompt, x_sample):
        bsz, s, _ = x.shape
        lp = s + SEQ_PAD
        for b in range(bsz):
            blocks += [meta, x[b].astype(F32), pad]
        nc = lp // HG_CHUNK_ROWS
        for _ in range(bsz):
            for c in range(nc):
                info += [c, nc, s + N_META]
        trunks.append((bsz, s, lp, row_off))
        row_off += bsz * lp
    h = jnp.concatenate(blocks, axis=0)
    hb, r = _row_scale(h, norm_ffn1[0].reshape(1, d_model).astype(F32))
    info = jnp.asarray(info, jnp.int32)
    tabs = [jnp.tile(_rope_table(lp), (bsz, 1)) for (bsz, s, lp, _) in trunks]

    sizes = (hg_dim,) * 5 + (q_lora, kv_lora, QK_ROPE, d_model, d_model)
    offs = [0]
    for sz in sizes:
        offs.append(offs[-1] + sz)

    for l in range(depth):
        ffn1 = _ffn_weights(w_ffn1_in[l], w_ffn1_out[l])
        ffn2 = _ffn_weights(w_ffn2_in[l], w_ffn2_out[l])
        wl = w_in[l]
        w_hg = wl[:, :offs[5]].astype(BF16)
        w_kr = wl[:, offs[7]:offs[8]]
        w_lat = jnp.concatenate([wl[:, offs[5]:offs[7]], w_kr, _swap_halves(w_kr, 1)], axis=1).astype(BF16)
        w_gh = wl[:, offs[8]:offs[9]].astype(BF16)
        w_ga = wl[:, offs[9]:offs[10]].astype(BF16)
        after_ffn2 = norm_ffn1[l + 1] if l + 1 < depth else jnp.ones((d_model,), F32)
        wq = w_uq[l].reshape(q_lora, mla_heads, QK_HEAD)
        wq = jnp.concatenate([wq, _swap_halves(wq[:, :, QK_NOPE:], 2)], axis=2)
        wq = wq.transpose(1, 0, 2).astype(BF16)
        wkv = w_ukv[l].reshape(kv_lora, mla_heads, QK_NOPE + V_HEAD).transpose(1, 0, 2).astype(BF16)

        h, hb, r = _ffn(h, hb, r, ffn1, norm_mix[l])

        hg = _matmul(hb, r, w_hg, head_major=True)
        lat = _matmul(hb, r, w_lat)
        o_h = _hgrn(hg, info, hgrn_lb_logits, hg_out_norm[l], l, hg_heads)

        o_a = []
        for (bsz, s, lp, off), tab in zip(trunks, tabs):
            q, k, v = _qkv_proj(lat, q_lat_norm[l].reshape(1, q_lora).astype(F32),
                                kv_lat_norm[l].reshape(1, kv_lora).astype(F32), wq, wkv,
                                _head_gain(q_head_norm[l]), _head_gain(k_head_norm[l]),
                                tab, off, bsz * lp, q_lora, kv_lora)
            o_a.append(_attention(q, k, v, bsz, lp, s + N_META))
        o_a = jnp.concatenate(o_a, axis=0)

        merged = _merge(hb, r, o_h, o_a, w_gh, w_ga, w_branch_hgrn[l].astype(BF16), w_branch_mla[l].astype(BF16))
        h, hb, r = _matmul_residual(merged, w_out[l].astype(BF16), h, 1.0, norm_ffn2[l])

        h, hb, r = _ffn(h, hb, r, ffn2, after_ffn2)

    outs = []
    for (bsz, s, lp, off) in trunks:
        outs.append(h[off:off + bsz * lp].reshape(bsz, lp, d_model)[:, N_META:N_META + s])
    return tuple(outs)
```

```python
import functools
import math

import jax
import jax.numpy as jnp
from jax import lax
from jax.experimental import pallas as pl
from jax.experimental.pallas import tpu as pltpu

N_META = 16
HG_HEAD_DIM = 128
QK_NOPE = 128
QK_ROPE = 64
QK_HEAD = QK_NOPE + QK_ROPE
V_HEAD = 128
ROPE_THETA = 10000.0
RMS_EPS = 1e-6

LANES = 128
SEQ_PAD = 128
HG_CHUNK_ROWS = 128
ROW_TILES = (1024, 512, 256, 128, 64, 32, 16)
HG_SUB_ROWS = 32
HG_EXACT_SUB_ROWS = 16
HG_MAX_LOG2_GROWTH = 115.0
ATTN_MAX_Q_ROWS = 640
QK_PAD = 256
RES_LONG_K = 8192
RES_LONG_K_TILE = 22 * LANES
VMEM_LIMIT = 56 * 1024 * 1024
VMEM_LIMIT_LARGE = 58 * 1024 * 1024
F32 = jnp.float32
BF16 = jnp.bfloat16
NEG_BIG = -1e30
LN2 = math.log(2.0)
LOG2E = 1.0 / LN2


def _cparams(sem, vmem_limit=VMEM_LIMIT):
    return pltpu.CompilerParams(dimension_semantics=sem, vmem_limit_bytes=vmem_limit)


def _pick(n, prefs):
    for p in prefs:
        if n % p == 0:
            return p
    return n


def _sigmoid(x):
    return 1.0 / (1.0 + jnp.exp(-x))


def _row_scale_kernel(x_ref, w_ref, hb_ref, r_ref):
    x = x_ref[...]
    hb_ref[...] = (x * w_ref[...]).astype(BF16)
    r_ref[...] = lax.rsqrt(jnp.mean(x * x, axis=-1, keepdims=True) + RMS_EPS)


def _row_scale(x, w):
    m, d = x.shape
    tm = _pick(m, (512, 256, 128, 64, 32, 16, 8))
    return pl.pallas_call(
        _row_scale_kernel,
        grid=(m // tm,),
        in_specs=[pl.BlockSpec((tm, d), lambda i: (i, 0)),
                  pl.BlockSpec((1, d), lambda i: (0, 0))],
        out_specs=[pl.BlockSpec((tm, d), lambda i: (i, 0)),
                   pl.BlockSpec((tm, 1), lambda i: (i, 0))],
        out_shape=[jax.ShapeDtypeStruct((m, d), BF16), jax.ShapeDtypeStruct((m, 1), F32)],
        compiler_params=_cparams(("parallel",)),
        name="row_scale",
    )(x, w)


def _mm_kernel(a_ref, r_ref, b_ref, o_ref):
    o_ref[...] = jnp.dot(a_ref[...], b_ref[...], preferred_element_type=F32) * r_ref[...]


def _mm_headmajor_kernel(a_ref, r_ref, b_ref, o_ref):
    acc = jnp.dot(a_ref[...], b_ref[...], preferred_element_type=F32) * r_ref[...]
    for j in range(o_ref.shape[0]):
        o_ref[j] = acc[:, j * LANES:(j + 1) * LANES]


def _matmul(a, r, b, *, head_major=False):
    m, k = a.shape
    n = b.shape[1]
    if n % 256 == 0:
        tm = _pick(m, ROW_TILES)
        tn = _pick(n, (1024, 512, 256))
    else:
        tm = _pick(m, (512, 256, 128, 64, 32, 16))
        tn = n
    grid = (m // tm, n // tn)
    in_specs = [pl.BlockSpec((tm, k), lambda i, j: (i, 0)),
                pl.BlockSpec((tm, 1), lambda i, j: (i, 0)),
                pl.BlockSpec((k, tn), lambda i, j: (0, j))]
    if head_major:
        return pl.pallas_call(
            _mm_headmajor_kernel, grid=grid, in_specs=in_specs,
            out_specs=pl.BlockSpec((tn // LANES, tm, LANES), lambda i, j: (j, i, 0)),
            out_shape=jax.ShapeDtypeStruct((n // LANES, m, LANES), F32),
            compiler_params=_cparams(("parallel", "arbitrary")),
            name="proj_headmajor",
        )(a, r, b)
    return pl.pallas_call(
        _mm_kernel, grid=grid, in_specs=in_specs,
        out_specs=pl.BlockSpec((tm, tn), lambda i, j: (i, j)),
        out_shape=jax.ShapeDtypeStruct((m, n), F32),
        compiler_params=_cparams(("parallel", "arbitrary")),
        name="proj",
    )(a, r, b)


def _ffn_in_kernel(x_ref, r_ref, wg_ref, wu_ref, o_ref):
    x = x_ref[...]
    r = r_ref[...]
    g = jnp.dot(x, wg_ref[...], preferred_element_type=F32) * r
    u = jnp.dot(x, wu_ref[...], preferred_element_type=F32) * r
    o_ref[...] = ((g * _sigmoid(g)) * u).astype(o_ref.dtype)


def _ffn_in(x, r, wg, wu):
    m, k = x.shape
    f = wg.shape[1]
    tm = _pick(m, ROW_TILES)
    tf = min(512, f)
    return pl.pallas_call(
        _ffn_in_kernel,
        grid=(m // tm, pl.cdiv(f, tf)),
        in_specs=[pl.BlockSpec((tm, k), lambda i, j: (i, 0)),
                  pl.BlockSpec((tm, 1), lambda i, j: (i, 0)),
                  pl.BlockSpec((k, tf), lambda i, j: (0, j)),
                  pl.BlockSpec((k, tf), lambda i, j: (0, j))],
        out_specs=pl.BlockSpec((tm, tf), lambda i, j: (i, j)),
        out_shape=jax.ShapeDtypeStruct((m, f), BF16),
        compiler_params=_cparams(("parallel", "arbitrary")),
        name="ffn_in",
    )(x, r, wg, wu)


def _mm_res_kernel(a_ref, b_ref, res_ref, w_ref, o_ref, hb_ref, rs_ref, ss_ref, *, scale, k_tail, n_k):
    j, kk = pl.program_id(1), pl.program_id(2)
    last = n_k - 1

    if n_k > 1:
        @pl.when(kk == 0)
        def _():
            o_ref[...] = res_ref[...] + scale * jnp.dot(a_ref[...], b_ref[...], preferred_element_type=F32)

        @pl.when(jnp.logical_and(kk > 0, kk < last))
        def _():
            o_ref[...] += scale * jnp.dot(a_ref[...], b_ref[...], preferred_element_type=F32)

    @pl.when(kk == last)
    def _():
        a, b = a_ref[...], b_ref[...]
        if k_tail:
            a = jnp.where(lax.broadcasted_iota(jnp.int32, a.shape, 1) < k_tail, a, jnp.zeros_like(a))
            b = jnp.where(lax.broadcasted_iota(jnp.int32, b.shape, 0) < k_tail, b, jnp.zeros_like(b))
        base = o_ref[...] if n_k > 1 else res_ref[...]
        o = base + scale * jnp.dot(a, b, preferred_element_type=F32)
        o_ref[...] = o
        hb_ref[...] = (o * w_ref[...]).astype(BF16)
        part = jnp.sum(o * o, axis=-1, keepdims=True)
        ss = jnp.where(j == 0, part, ss_ref[...] + part)
        ss_ref[...] = ss
        d_model = o.shape[1] * pl.num_programs(1)
        rs_ref[...] = lax.rsqrt(ss * (1.0 / d_model) + RMS_EPS)


def _matmul_residual(a, b, res, scale, next_gain):
    m, k = a.shape
    n = b.shape[1]
    tm = _pick(m, ROW_TILES)
    if k > RES_LONG_K:
        tn, tk = _pick(n, (1024, 512, 256, 128)), RES_LONG_K_TILE
    else:
        tn, tk = _pick(n, (2048, 1024, 512, 256, 128)), min(1024, k)
    return pl.pallas_call(
        functools.partial(_mm_res_kernel, scale=scale, k_tail=k % tk, n_k=pl.cdiv(k, tk)),
        grid=(m // tm, n // tn, pl.cdiv(k, tk)),
        in_specs=[pl.BlockSpec((tm, tk), lambda i, j, kk: (i, kk)),
                  pl.BlockSpec((tk, tn), lambda i, j, kk: (kk, j)),
                  pl.BlockSpec((tm, tn), lambda i, j, kk: (i, j)),
                  pl.BlockSpec((1, tn), lambda i, j, kk: (0, j))],
        out_specs=[pl.BlockSpec((tm, tn), lambda i, j, kk: (i, j)),
                   pl.BlockSpec((tm, tn), lambda i, j, kk: (i, j)),
                   pl.BlockSpec((tm, 1), lambda i, j, kk: (i, 0))],
        out_shape=[jax.ShapeDtypeStruct((m, n), F32), jax.ShapeDtypeStruct((m, n), BF16),
                   jax.ShapeDtypeStruct((m, 1), F32)],
        scratch_shapes=[pltpu.VMEM((tm, 1), F32)],
        compiler_params=_cparams(("parallel", "arbitrary", "arbitrary"), VMEM_LIMIT_LARGE),
        name="proj_residual",
    )(a, b, res, next_gain.reshape(1, n).astype(F32))


def _merge_kernel(u_ref, r_ref, oh_ref, oa_ref, wgh_ref, wga_ref, wbh_ref, wba_ref, o_ref):
    u = u_ref[...]
    r = r_ref[...]
    oh = jnp.concatenate([oh_ref[h] for h in range(oh_ref.shape[0])], axis=1)
    gh = jnp.dot(u, wgh_ref[...], preferred_element_type=F32) * r
    ga = jnp.dot(u, wga_ref[...], preferred_element_type=F32) * r
    bh = jnp.dot(oh, wbh_ref[...], preferred_element_type=F32)
    ba = jnp.dot(oa_ref[...], wba_ref[...], preferred_element_type=F32)
    o_ref[...] = (_sigmoid(gh) * bh + _sigmoid(ga) * ba).astype(o_ref.dtype)


def _merge(u, r, oh, oa, wgh, wga, wbh, wba):
    m, d = u.shape
    nh = oh.shape[0]
    n = wgh.shape[1]
    tm = _pick(m, ROW_TILES)
    tn = _pick(n, (256, 128))
    return pl.pallas_call(
        _merge_kernel,
        grid=(m // tm, n // tn),
        in_specs=[pl.BlockSpec((tm, d), lambda i, j: (i, 0)),
                  pl.BlockSpec((tm, 1), lambda i, j: (i, 0)),
                  pl.BlockSpec((nh, tm, LANES), lambda i, j: (0, i, 0)),
                  pl.BlockSpec((tm, oa.shape[1]), lambda i, j: (i, 0)),
                  pl.BlockSpec((d, tn), lambda i, j: (0, j)),
                  pl.BlockSpec((d, tn), lambda i, j: (0, j)),
                  pl.BlockSpec((wbh.shape[0], tn), lambda i, j: (0, j)),
                  pl.BlockSpec((wba.shape[0], tn), lambda i, j: (0, j))],
        out_specs=pl.BlockSpec((tm, tn), lambda i, j: (i, j)),
        out_shape=jax.ShapeDtypeStruct((m, n), BF16),
        compiler_params=_cparams(("parallel", "arbitrary")),
        name="branch_merge",
    )(u, r, oh, oa, wgh, wga, wbh, wba)


def _rope_tail(y2, tab):
    w = y2 * tab
    rot = w + pltpu.roll(w, QK_ROPE, axis=1)
    lane = lax.broadcasted_iota(jnp.int32, rot.shape, 1)
    return jnp.where(lane < QK_ROPE, rot, 0.0)


def _rope_sumsq(y2):
    lane = lax.broadcasted_iota(jnp.int32, y2.shape, 1)
    y2m = jnp.where(lane < QK_ROPE, y2, 0.0)
    return jnp.sum(y2m * y2m, axis=-1, keepdims=True)


def _head_rms_scale(y, ones_mask, extra_ss):
    ss = jnp.dot((y * y).astype(BF16), ones_mask, preferred_element_type=F32)
    return lax.rsqrt((ss + extra_ss) * (1.0 / QK_HEAD) + RMS_EPS)


def _latent_norm(x, w):
    ms = jnp.mean(x * x, axis=-1, keepdims=True)
    return ((x * lax.rsqrt(ms + RMS_EPS)) * w).astype(BF16)


def _qkv_kernel(lat_ref, qlw_ref, kvlw_ref, wq_ref, wkv_ref, gq_ref, gk_ref, tab_ref,
                q_ref, k_ref, v_ref, *, q_lora, kv_lora, q_scale):
    xq = _latent_norm(lat_ref[:, 0:q_lora], qlw_ref[...])
    xkv = _latent_norm(lat_ref[:, q_lora:q_lora + kv_lora], kvlw_ref[...])
    kr = lat_ref[:, q_lora + kv_lora:q_lora + kv_lora + LANES]
    gq1 = gq_ref[:, :QK_NOPE] * q_scale
    gk1 = gk_ref[:, :QK_NOPE]
    tab_q = tab_ref[...] * (gq_ref[:, QK_NOPE:] * q_scale)
    tab_k = tab_ref[...] * gk_ref[:, QK_NOPE:]
    kr_ss = _rope_sumsq(kr)
    row = lax.broadcasted_iota(jnp.int32, (QK_PAD, LANES), 0)
    ones_q = jnp.where(row < QK_HEAD, 1.0, 0.0).astype(BF16)
    ones_k = jnp.ones((QK_NOPE, LANES), BF16)

    def body(h, carry):
        y = jnp.dot(xq, wq_ref[h], preferred_element_type=F32)
        y1, y2 = y[:, :QK_NOPE], y[:, QK_NOPE:]
        r = _head_rms_scale(y, ones_q, 0.0)
        q_ref[h, :, 0:QK_NOPE] = ((y1 * r) * gq1).astype(BF16)
        q_ref[h, :, QK_NOPE:] = _rope_tail(y2 * r, tab_q).astype(BF16)

        z = jnp.dot(xkv, wkv_ref[h], preferred_element_type=F32)
        z1 = z[:, :QK_NOPE]
        rk = _head_rms_scale(z1, ones_k, kr_ss)
        k_ref[h, :, 0:QK_NOPE] = ((z1 * rk) * gk1).astype(BF16)
        k_ref[h, :, QK_NOPE:] = _rope_tail(kr * rk, tab_k).astype(BF16)
        v_ref[h] = z[:, QK_NOPE:].astype(BF16)
        return carry

    lax.fori_loop(0, q_ref.shape[0], body, 0, unroll=2)


def _qkv_proj(lat, q_lat_w, kv_lat_w, wq, wkv, gq, gk, tab, row_off, rows, q_lora, kv_lora):
    nh = wq.shape[0]
    tl = _pick(math.gcd(rows, row_off) if row_off else rows, (512, 256, 128, 64, 32, 16))
    off = row_off // tl
    full = lambda shape: pl.BlockSpec(shape, lambda i: (0,) * len(shape))
    return pl.pallas_call(
        functools.partial(_qkv_kernel, q_lora=q_lora, kv_lora=kv_lora,
                          q_scale=LOG2E / math.sqrt(QK_HEAD)),
        grid=(rows // tl,),
        in_specs=[pl.BlockSpec((tl, lat.shape[1]), lambda i: (off + i, 0)),
                  full((1, q_lora)), full((1, kv_lora)),
                  full((nh, q_lora, QK_PAD)), full((nh, kv_lora, QK_NOPE + V_HEAD)),
                  full((1, QK_PAD)), full((1, QK_PAD)),
                  pl.BlockSpec((tl, LANES), lambda i: (i, 0))],
        out_specs=[pl.BlockSpec((nh, tl, QK_PAD), lambda i: (0, i, 0)),
                   pl.BlockSpec((nh, tl, QK_PAD), lambda i: (0, i, 0)),
                   pl.BlockSpec((nh, tl, V_HEAD), lambda i: (0, i, 0))],
        out_shape=[jax.ShapeDtypeStruct((nh, rows, QK_PAD), BF16),
                   jax.ShapeDtypeStruct((nh, rows, QK_PAD), BF16),
                   jax.ShapeDtypeStruct((nh, rows, V_HEAD), BF16)],
        compiler_params=_cparams(("parallel",)),
        name="mla_qkv_proj",
    )(lat, q_lat_w, kv_lat_w, wq, wkv, gq, gk, tab)


def _attn_kernel(q_ref, k_ref, v_ref, o_ref, *, n_valid):
    q = q_ref[0]
    lp = k_ref.shape[1]
    kb = max(QK_PAD, 1 << ((lp // 2).bit_length() - 1))
    edges = list(range(0, min(lp, n_valid), kb)) + [min(lp, -(-n_valid // LANES) * LANES)]
    m = denom = acc = None
    for c0, c1 in zip(edges[:-1], edges[1:]):
        s = lax.dot_general(q, k_ref[0, c0:c1, :], (((1,), (1,)), ((), ())), preferred_element_type=F32)
        if c1 > n_valid:
            col = lax.broadcasted_iota(jnp.int32, s.shape, 1) + c0
            s = jnp.where(col < n_valid, s, NEG_BIG)
        mb = jnp.max(s, axis=-1, keepdims=True)
        if m is None:
            m = mb
            p = jnp.exp2(s - m)
            denom = jnp.sum(p, axis=-1, keepdims=True)
            acc = jnp.dot(p.astype(BF16), v_ref[0, c0:c1, :], preferred_element_type=F32)
        else:
            m_new = jnp.maximum(m, mb)
            alpha = jnp.exp2(m - m_new)
            p = jnp.exp2(s - m_new)
            denom = alpha * denom + jnp.sum(p, axis=-1, keepdims=True)
            acc = alpha * acc + jnp.dot(p.astype(BF16), v_ref[0, c0:c1, :], preferred_element_type=F32)
            m = m_new
    o_ref[...] = (acc / denom).astype(o_ref.dtype)


def _attention(q, k, v, batch, lp, n_valid):
    nh = q.shape[0]
    nq = next(n for n in range(1, lp + 1) if lp % n == 0 and (lp // n) % 16 == 0 and lp // n <= ATTN_MAX_Q_ROWS)
    tq = lp // nq
    return pl.pallas_call(
        functools.partial(_attn_kernel, n_valid=n_valid),
        grid=(batch, nh, nq),
        in_specs=[pl.BlockSpec((1, tq, QK_PAD), lambda b, h, i: (h, b * nq + i, 0)),
                  pl.BlockSpec((1, lp, QK_PAD), lambda b, h, i: (h, b, 0)),
                  pl.BlockSpec((1, lp, V_HEAD), lambda b, h, i: (h, b, 0))],
        out_specs=pl.BlockSpec((tq, V_HEAD), lambda b, h, i: (b * nq + i, h)),
        out_shape=jax.ShapeDtypeStruct((batch * lp, nh * V_HEAD), BF16),
        compiler_params=_cparams(("parallel", "parallel", "arbitrary")),
        name="mla_attention",
    )(q, k, v)


def _lower_bound(logit_ref, h, layer):
    lg = logit_ref[0, h]
    e = jnp.exp(lg - jnp.max(lg, axis=0, keepdims=True))
    p = e / jnp.sum(e, axis=0, keepdims=True)
    lb = jnp.zeros((1, lg.shape[1]), F32)
    for j in range(1, layer + 1):
        lb = lb + p[j:j + 1]
    return lb


def _gates(zq, zf, zi, lb, row_ok):
    q = zq * _sigmoid(zq)
    e = jnp.exp(-jnp.abs(zf))
    r = 1.0 / (1.0 + e)
    er = e * r
    pos = zf >= 0.0
    k = jnp.where(row_ok, (1.0 - lb) * jnp.where(pos, er, r), 0.0)
    f = lb + (1.0 - lb) * jnp.where(pos, r, er)
    g = jnp.maximum(jnp.log(f) * LOG2E, (jnp.log1p(-lb) * LOG2E - 1.0) + jnp.minimum(zf, 0.0) * LOG2E)
    return q, k, zi, g


def _log_cumsum(g, tri):
    g1 = g.astype(BF16)
    r1 = g - g1.astype(F32)
    g2 = r1.astype(BF16)
    g3 = (r1 - g2.astype(F32)).astype(BF16)
    bb = jnp.dot(tri, jnp.concatenate([g1, g2, g3], axis=1), preferred_element_type=F32)
    return (bb[:, :LANES] + bb[:, LANES:2 * LANES]) + bb[:, 2 * LANES:]


def _carried(q, k, v, b, total, st):
    qh = (q * jnp.exp2(b)).astype(BF16)
    kh = (k * jnp.exp2(total - b)).astype(BF16)
    o = lax.dot_general(qh, st.astype(BF16), (((1,), (1,)), ((), ())), preferred_element_type=F32)
    st_new = st * jnp.exp2(total) + lax.dot_general(
        v.astype(BF16), kh, (((0,), (0,)), ((), ())), preferred_element_type=F32)
    return o, st_new


def _hgrn_chunk_factorised(q, k, v, g, st, tri, in_scan, reverse):
    c_rows = q.shape[0]
    sub = HG_SUB_ROWS
    b = _log_cumsum(g, tri)
    total = b[0:1] if reverse else b[c_rows - 1:c_rows]
    o, st_new = _carried(q, k, v, b, total, st)

    zero_row = jnp.zeros((1, LANES), F32)
    growth = zero_row
    rows = []
    for blk in range(c_rows // sub):
        lo, hi = blk * sub, (blk + 1) * sub
        if reverse:
            ref = b[hi:hi + 1] if hi < c_rows else zero_row
            growth = jnp.maximum(growth, ref - b[lo:lo + 1])
        else:
            ref = b[lo - 1:lo] if lo > 0 else zero_row
            growth = jnp.maximum(growth, ref - b[hi - 1:hi])
        qi = (q[lo:hi] * jnp.exp2(b[lo:hi] - ref)).astype(BF16)
        ks = (k * jnp.exp2(jnp.minimum(ref - b, HG_MAX_LOG2_GROWTH))).astype(BF16)
        rows.append(lax.dot_general(qi, ks, (((1,), (1,)), ((), ())), preferred_element_type=F32))
    a = jnp.where(in_scan, jnp.concatenate(rows, axis=0), 0.0)
    o = o + jnp.dot(a.astype(BF16), v.astype(BF16), preferred_element_type=F32)
    return o, st_new, growth


def _hgrn_chunk_exact(q, k, v, g, st, tri, ones_sq, reverse):
    c_rows = q.shape[0]
    sub = HG_EXACT_SUB_ROWS
    nb = c_rows // sub
    b = _log_cumsum(g, tri)
    total = b[0:1] if reverse else b[c_rows - 1:c_rows]
    o, _ = _carried(q, k, v, b, total, st)

    vb = v.astype(BF16)
    pieces = []
    for blk in range(nb):
        lo, hi = blk * sub, (blk + 1) * sub
        if reverse:
            if blk == nb - 1:
                pieces.append(jnp.zeros((sub, LANES), F32))
                continue
            ref = b[hi:hi + 1]
            other = slice(hi, c_rows)
        else:
            if blk == 0:
                pieces.append(jnp.zeros((sub, LANES), F32))
                continue
            ref = b[lo - 1:lo]
            other = slice(0, lo)
        qi = (q[lo:hi] * jnp.exp2(b[lo:hi] - ref)).astype(BF16)
        ko = (k[other] * jnp.exp2(ref - b[other])).astype(BF16)
        a = lax.dot_general(qi, ko, (((1,), (1,)), ((), ())), preferred_element_type=F32)
        pieces.append(jnp.dot(a.astype(BF16), vb[other], preferred_element_type=F32))
    o = o + jnp.concatenate(pieces, axis=0)

    pos = lax.broadcasted_iota(jnp.int32, (c_rows, LANES), 0) % sub
    d_list, v_list = [], []
    for d in range(sub):
        if d == 0:
            vs = v
            dmat = q * k
        else:
            shift = (c_rows - d) if reverse else d
            ks = pltpu.roll(k, shift, axis=0)
            bs = pltpu.roll(b, shift, axis=0)
            vs = pltpu.roll(v, shift, axis=0)
            ok = (pos + d < sub) if reverse else (pos >= d)
            dmat = (q * ks) * jnp.exp2(jnp.where(ok, b - bs, NEG_BIG))
        d_list.append(dmat.astype(BF16))
        v_list.append(vs)
    rs = jnp.dot(jnp.concatenate(d_list, axis=0), ones_sq, preferred_element_type=F32)
    for d in range(sub):
        o = o + rs[d * c_rows:(d + 1) * c_rows] * v_list[d]
    return o


def _hgrn_kernel(info_ref, *refs, layer, reverse):
    if reverse:
        zq_ref, zf_ref, zi_ref, zg_ref, of_ref, lg_ref, nw_ref, o_ref, st_ref = refs
    else:
        zq_ref, zf_ref, zi_ref, lg_ref, o_ref, st_ref = refs
    step = pl.program_id(0)
    g = (pl.num_programs(0) - 1 - step) if reverse else step
    local, n_chunks, seq_rows = info_ref[3 * g], info_ref[3 * g + 1], info_ref[3 * g + 2]
    nh, c_rows = zq_ref.shape[0], zq_ref.shape[1]
    cur = step % 2

    @pl.when((local == n_chunks - 1) if reverse else (local == 0))
    def _():
        st_ref[cur] = jnp.zeros(st_ref.shape[1:], F32)

    rr = lax.broadcasted_iota(jnp.int32, (c_rows, c_rows), 0)
    cc = lax.broadcasted_iota(jnp.int32, (c_rows, c_rows), 1)
    in_scan = (cc >= rr) if reverse else (cc <= rr)
    tri = jnp.where(in_scan, 1.0, 0.0).astype(BF16)
    row_ok = (lax.broadcasted_iota(jnp.int32, (c_rows, LANES), 0) + local * c_rows) < seq_rows

    def finish(h, o):
        if reverse:
            o = o + of_ref[h]
            ms = jnp.mean(o * o, axis=-1, keepdims=True)
            zg = zg_ref[h]
            o_ref[h] = (((o * lax.rsqrt(ms + RMS_EPS)) * nw_ref[...]) * (zg * _sigmoid(zg))).astype(o_ref.dtype)
        else:
            o_ref[h] = o

    def fast_body(h, growth):
        q, k, v, gl = _gates(zq_ref[h], zf_ref[h], zi_ref[h], _lower_bound(lg_ref, h, layer), row_ok)
        o, st_new, gr = _hgrn_chunk_factorised(q, k, v, gl, st_ref[cur, h], tri, in_scan, reverse)
        st_ref[1 - cur, h] = st_new
        finish(h, o)
        return jnp.maximum(growth, gr)

    growth = lax.fori_loop(0, nh, fast_body, jnp.zeros((1, LANES), F32), unroll=16)

    @pl.when(jnp.max(growth) > HG_MAX_LOG2_GROWTH)
    def _():
        ones_sq = jnp.ones((LANES, LANES), BF16)

        def exact_body(h, carry):
            q, k, v, gl = _gates(zq_ref[h], zf_ref[h], zi_ref[h], _lower_bound(lg_ref, h, layer), row_ok)
            finish(h, _hgrn_chunk_exact(q, k, v, gl, st_ref[cur, h], tri, ones_sq, reverse))
            return carry

        lax.fori_loop(0, nh, exact_body, 0)


def _hgrn(hg, info, lb_logits, out_norm, layer, nh):
    m = hg.shape[1]
    c = HG_CHUNK_ROWS
    n_chunks = m // c
    depth = lb_logits.shape[1]
    lg = lb_logits.astype(F32).reshape(2, depth, nh, LANES).transpose(0, 2, 1, 3)

    def blk(grp, rev):
        if rev:
            return pl.BlockSpec((nh, c, LANES), lambda g, info_ref: (grp, n_chunks - 1 - g, 0))
        return pl.BlockSpec((nh, c, LANES), lambda g, info_ref: (grp, g, 0))

    state = pltpu.VMEM((2, nh, LANES, LANES), F32)
    o_f = pl.pallas_call(
        functools.partial(_hgrn_kernel, layer=layer, reverse=False),
        grid_spec=pltpu.PrefetchScalarGridSpec(
            num_scalar_prefetch=1, grid=(n_chunks,),
            in_specs=[blk(0, False), blk(1, False), blk(3, False),
                      pl.BlockSpec((1, nh, depth, LANES), lambda g, info_ref: (0, 0, 0, 0))],
            out_specs=blk(0, False),
            scratch_shapes=[state]),
        out_shape=jax.ShapeDtypeStruct((nh, m, LANES), F32),
        compiler_params=_cparams(("arbitrary",)),
        name="hgrn_forward",
    )(info, hg, hg, hg, lg)

    return pl.pallas_call(
        functools.partial(_hgrn_kernel, layer=layer, reverse=True),
        grid_spec=pltpu.PrefetchScalarGridSpec(
            num_scalar_prefetch=1, grid=(n_chunks,),
            in_specs=[blk(0, True), blk(2, True), blk(3, True), blk(4, True), blk(0, True),
                      pl.BlockSpec((1, nh, depth, LANES), lambda g, info_ref: (1, 0, 0, 0)),
                      pl.BlockSpec((1, LANES), lambda g, info_ref: (0, 0))],
            out_specs=blk(0, True),
            scratch_shapes=[state]),
        out_shape=jax.ShapeDtypeStruct((nh, m, LANES), BF16),
        compiler_params=_cparams(("arbitrary",)),
        name="hgrn_backward",
    )(info, hg, hg, hg, hg, o_f, lg, out_norm.reshape(1, LANES).astype(F32))


def _swap_halves(x, axis):
    a, b = jnp.split(x, 2, axis=axis)
    return jnp.concatenate([b, a], axis=axis)


def _rope_table(rows):
    inv_freq = 1.0 / (ROPE_THETA ** (jnp.arange(0, QK_ROPE, 2, dtype=F32) / QK_ROPE))
    ang = jnp.arange(rows, dtype=F32)[:, None] * inv_freq[None, :]
    c, s = jnp.cos(ang), jnp.sin(ang)
    return jnp.concatenate([c, c, -s, s], axis=1)


def _head_gain(g):
    g = g.astype(F32)
    return jnp.concatenate([g[:QK_NOPE], g[QK_NOPE:], _swap_halves(g[QK_NOPE:], 0)]).reshape(1, QK_PAD)


def _ffn_weights(w_in, w_out):
    d_ff = w_out.shape[0]
    return w_in[:, :d_ff].astype(BF16), w_in[:, d_ff:].astype(BF16), w_out.astype(BF16)


def _ffn(h, hb, r, weights, next_gain, scale=0.5):
    wg, wu, wo = weights
    return _matmul_residual(_ffn_in(hb, r, wg, wu), wo, h, scale, next_gain)


def kernel(x_prompt, x_sample, meta_tokens, hgrn_lb_logits, norm_ffn1, w_ffn1_in, w_ffn1_out, norm_mix, w_in,
           q_lat_norm, w_uq, kv_lat_norm, w_ukv, q_head_norm, k_head_norm, hg_out_norm,
           w_branch_hgrn, w_branch_mla, w_out, norm_ffn2, w_ffn2_in, w_ffn2_out):
    depth = norm_ffn1.shape[0]
    d_model = x_prompt.shape[-1]
    hg_dim = hgrn_lb_logits.shape[-1]
    hg_heads = hg_dim // HG_HEAD_DIM
    q_lora = q_lat_norm.shape[-1]
    kv_lora = kv_lat_norm.shape[-1]
    mla_heads = w_uq.shape[-1] // QK_HEAD

    trunks = []
    row_off = 0
    blocks, info = [], []
    meta = meta_tokens.astype(F32)
    pad = jnp.zeros((SEQ_PAD - N_META, d_model), F32)
    for x in (x_prompt, x_sample):
        bsz, s, _ = x.shape
        lp = s + SEQ_PAD
        for b in range(bsz):
            blocks += [meta, x[b].astype(F32), pad]
        nc = lp // HG_CHUNK_ROWS
        for _ in range(bsz):
            for c in range(nc):
                info += [c, nc, s + N_META]
        trunks.append((bsz, s, lp, row_off))
        row_off += bsz * lp
    h = jnp.concatenate(blocks, axis=0)
    hb, r = _row_scale(h, norm_ffn1[0].reshape(1, d_model).astype(F32))
    info = jnp.asarray(info, jnp.int32)
    tabs = [jnp.tile(_rope_table(lp), (bsz, 1)) for (bsz, s, lp, _) in trunks]

    sizes = (hg_dim,) * 5 + (q_lora, kv_lora, QK_ROPE, d_model, d_model)
    offs = [0]
    for sz in sizes:
        offs.append(offs[-1] + sz)

    for l in range(depth):
        ffn1 = _ffn_weights(w_ffn1_in[l], w_ffn1_out[l])
        ffn2 = _ffn_weights(w_ffn2_in[l], w_ffn2_out[l])
        wl = w_in[l]
        w_hg = wl[:, :offs[5]].astype(BF16)
        w_kr = wl[:, offs[7]:offs[8]]
        w_lat = jnp.concatenate([wl[:, offs[5]:offs[7]], w_kr, _swap_halves(w_kr, 1)], axis=1).astype(BF16)
        w_gh = wl[:, offs[8]:offs[9]].astype(BF16)
        w_ga = wl[:, offs[9]:offs[10]].astype(BF16)
        after_ffn2 = norm_ffn1[l + 1] if l + 1 < depth else jnp.ones((d_model,), F32)
        wq = w_uq[l].reshape(q_lora, mla_heads, QK_HEAD)
        wq = jnp.concatenate([wq, _swap_halves(wq[:, :, QK_NOPE:], 2)], axis=2)
        wq = wq.transpose(1, 0, 2).astype(BF16)
        wkv = w_ukv[l].reshape(kv_lora, mla_heads, QK_NOPE + V_HEAD).transpose(1, 0, 2).astype(BF16)

        h, hb, r = _ffn(h, hb, r, ffn1, norm_mix[l])

        hg = _matmul(hb, r, w_hg, head_major=True)
        lat = _matmul(hb, r, w_lat)
        o_h = _hgrn(hg, info, hgrn_lb_logits, hg_out_norm[l], l, hg_heads)

        o_a = []
        for (bsz, s, lp, off), tab in zip(trunks, tabs):
            q, k, v = _qkv_proj(lat, q_lat_norm[l].reshape(1, q_lora).astype(F32),
                                kv_lat_norm[l].reshape(1, kv_lora).astype(F32), wq, wkv,
                                _head_gain(q_head_norm[l]), _head_gain(k_head_norm[l]),
                                tab, off, bsz * lp, q_lora, kv_lora)
            o_a.append(_attention(q, k, v, bsz, lp, s + N_META))
        o_a = jnp.concatenate(o_a, axis=0)

        merged = _merge(hb, r, o_h, o_a, w_gh, w_ga, w_branch_hgrn[l].astype(BF16), w_branch_mla[l].astype(BF16))
        h, hb, r = _matmul_residual(merged, w_out[l].astype(BF16), h, 1.0, norm_ffn2[l])

        h, hb, r = _ffn(h, hb, r, ffn2, after_ffn2)

    outs = []
    for (bsz, s, lp, off) in trunks:
        outs.append(h[off:off + bsz * lp].reshape(bsz, lp, d_model)[:, N_META:N_META + s])
    return tuple(outs)
```

```python
import functools
import math

import jax
import jax.numpy as jnp
from jax import lax
from jax.experimental import pallas as pl
from jax.experimental.pallas import tpu as pltpu

N_META = 16
HG_HEAD_DIM = 128
QK_NOPE = 128
QK_ROPE = 64
QK_HEAD = QK_NOPE + QK_ROPE
V_HEAD = 128
ROPE_THETA = 10000.0
RMS_EPS = 1e-6

LANES = 128
SEQ_PAD = 128
HG_CHUNK_ROWS = 128
ROW_TILES = (1024, 512, 256, 128, 64, 32, 16)
HG_SUB_ROWS = 32
HG_EXACT_SUB_ROWS = 16
HG_MAX_LOG2_GROWTH = 115.0
ATTN_MAX_Q_ROWS = 640
QK_PAD = 256
RES_LONG_K = 8192
RES_LONG_K_TILE = 29 * LANES
VMEM_LIMIT = 56 * 1024 * 1024
VMEM_LIMIT_LARGE = 58 * 1024 * 1024
F32 = jnp.float32
BF16 = jnp.bfloat16
NEG_BIG = -1e30
LN2 = math.log(2.0)
LOG2E = 1.0 / LN2


def _cparams(sem, vmem_limit=VMEM_LIMIT):
    return pltpu.CompilerParams(dimension_semantics=sem, vmem_limit_bytes=vmem_limit)


def _pick(n, prefs):
    for p in prefs:
        if n % p == 0:
            return p
    return n


def _sigmoid(x):
    return 1.0 / (1.0 + jnp.exp(-x))


def _row_scale_kernel(x_ref, w_ref, hb_ref, r_ref):
    x = x_ref[...]
    hb_ref[...] = (x * w_ref[...]).astype(BF16)
    r_ref[...] = lax.rsqrt(jnp.mean(x * x, axis=-1, keepdims=True) + RMS_EPS)


def _row_scale(x, w):
    m, d = x.shape
    tm = _pick(m, (512, 256, 128, 64, 32, 16, 8))
    return pl.pallas_call(
        _row_scale_kernel,
        grid=(m // tm,),
        in_specs=[pl.BlockSpec((tm, d), lambda i: (i, 0)),
                  pl.BlockSpec((1, d), lambda i: (0, 0))],
        out_specs=[pl.BlockSpec((tm, d), lambda i: (i, 0)),
                   pl.BlockSpec((tm, 1), lambda i: (i, 0))],
        out_shape=[jax.ShapeDtypeStruct((m, d), BF16), jax.ShapeDtypeStruct((m, 1), F32)],
        compiler_params=_cparams(("parallel",)),
        name="row_scale",
    )(x, w)


def _mm_kernel(a_ref, r_ref, b_ref, o_ref):
    o_ref[...] = jnp.dot(a_ref[...], b_ref[...], preferred_element_type=F32) * r_ref[...]


def _mm_headmajor_kernel(a_ref, r_ref, b_ref, o_ref):
    acc = jnp.dot(a_ref[...], b_ref[...], preferred_element_type=F32) * r_ref[...]
    for j in range(o_ref.shape[0]):
        o_ref[j] = acc[:, j * LANES:(j + 1) * LANES]


def _matmul(a, r, b, *, head_major=False):
    m, k = a.shape
    n = b.shape[1]
    if n % 256 == 0:
        tm = _pick(m, ROW_TILES)
        tn = _pick(n, (1024, 512, 256))
    else:
        tm = _pick(m, (512, 256, 128, 64, 32, 16))
        tn = n
    grid = (m // tm, n // tn)
    in_specs = [pl.BlockSpec((tm, k), lambda i, j: (i, 0)),
                pl.BlockSpec((tm, 1), lambda i, j: (i, 0)),
                pl.BlockSpec((k, tn), lambda i, j: (0, j))]
    if head_major:
        return pl.pallas_call(
            _mm_headmajor_kernel, grid=grid, in_specs=in_specs,
            out_specs=pl.BlockSpec((tn // LANES, tm, LANES), lambda i, j: (j, i, 0)),
            out_shape=jax.ShapeDtypeStruct((n // LANES, m, LANES), F32),
            compiler_params=_cparams(("parallel", "arbitrary")),
            name="proj_headmajor",
        )(a, r, b)
    return pl.pallas_call(
        _mm_kernel, grid=grid, in_specs=in_specs,
        out_specs=pl.BlockSpec((tm, tn), lambda i, j: (i, j)),
        out_shape=jax.ShapeDtypeStruct((m, n), F32),
        compiler_params=_cparams(("parallel", "arbitrary")),
        name="proj",
    )(a, r, b)


def _ffn_in_kernel(x_ref, r_ref, wg_ref, wu_ref, o_ref):
    x = x_ref[...]
    r = r_ref[...]
    g = jnp.dot(x, wg_ref[...], preferred_element_type=F32) * r
    u = jnp.dot(x, wu_ref[...], preferred_element_type=F32) * r
    o_ref[...] = ((g * _sigmoid(g)) * u).astype(o_ref.dtype)


def _ffn_in(x, r, wg, wu):
    m, k = x.shape
    f = wg.shape[1]
    tm = _pick(m, ROW_TILES)
    tf = min(512, f)
    return pl.pallas_call(
        _ffn_in_kernel,
        grid=(m // tm, pl.cdiv(f, tf)),
        in_specs=[pl.BlockSpec((tm, k), lambda i, j: (i, 0)),
                  pl.BlockSpec((tm, 1), lambda i, j: (i, 0)),
                  pl.BlockSpec((k, tf), lambda i, j: (0, j)),
                  pl.BlockSpec((k, tf), lambda i, j: (0, j))],
        out_specs=pl.BlockSpec((tm, tf), lambda i, j: (i, j)),
        out_shape=jax.ShapeDtypeStruct((m, f), BF16),
        compiler_params=_cparams(("parallel", "arbitrary")),
        name="ffn_in",
    )(x, r, wg, wu)


def _mm_res_kernel(a_ref, b_ref, res_ref, w_ref, o_ref, hb_ref, rs_ref, ss_ref, *, scale, k_tail, n_k):
    j, kk = pl.program_id(1), pl.program_id(2)
    last = n_k - 1

    if n_k > 1:
        @pl.when(kk == 0)
        def _():
            o_ref[...] = res_ref[...] + scale * jnp.dot(a_ref[...], b_ref[...], preferred_element_type=F32)

        @pl.when(jnp.logical_and(kk > 0, kk < last))
        def _():
            o_ref[...] += scale * jnp.dot(a_ref[...], b_ref[...], preferred_element_type=F32)

    @pl.when(kk == last)
    def _():
        a, b = a_ref[...], b_ref[...]
        if k_tail:
            a = jnp.where(lax.broadcasted_iota(jnp.int32, a.shape, 1) < k_tail, a, jnp.zeros_like(a))
            b = jnp.where(lax.broadcasted_iota(jnp.int32, b.shape, 0) < k_tail, b, jnp.zeros_like(b))
        base = o_ref[...] if n_k > 1 else res_ref[...]
        o = base + scale * jnp.dot(a, b, preferred_element_type=F32)
        o_ref[...] = o
        hb_ref[...] = (o * w_ref[...]).astype(BF16)
        part = jnp.sum(o * o, axis=-1, keepdims=True)
        ss = jnp.where(j == 0, part, ss_ref[...] + part)
        ss_ref[...] = ss
        d_model = o.shape[1] * pl.num_programs(1)
        rs_ref[...] = lax.rsqrt(ss * (1.0 / d_model) + RMS_EPS)


def _matmul_residual(a, b, res, scale, next_gain):
    m, k = a.shape
    n = b.shape[1]
    tm = _pick(m, ROW_TILES)
    if k > RES_LONG_K:
        tn, tk = _pick(n, (1024, 512, 256, 128)), RES_LONG_K_TILE
    else:
        tn, tk = _pick(n, (2048, 1024, 512, 256, 128)), min(1024, k)
    return pl.pallas_call(
        functools.partial(_mm_res_kernel, scale=scale, k_tail=k % tk, n_k=pl.cdiv(k, tk)),
        grid=(m // tm, n // tn, pl.cdiv(k, tk)),
        in_specs=[pl.BlockSpec((tm, tk), lambda i, j, kk: (i, kk)),
                  pl.BlockSpec((tk, tn), lambda i, j, kk: (kk, j)),
                  pl.BlockSpec((tm, tn), lambda i, j, kk: (i, j), pipeline_mode=pl.Buffered(1)),
                  pl.BlockSpec((1, tn), lambda i, j, kk: (0, j))],
        out_specs=[pl.BlockSpec((tm, tn), lambda i, j, kk: (i, j)),
                   pl.BlockSpec((tm, tn), lambda i, j, kk: (i, j)),
                   pl.BlockSpec((tm, 1), lambda i, j, kk: (i, 0))],
        out_shape=[jax.ShapeDtypeStruct((m, n), F32), jax.ShapeDtypeStruct((m, n), BF16),
                   jax.ShapeDtypeStruct((m, 1), F32)],
        scratch_shapes=[pltpu.VMEM((tm, 1), F32)],
        compiler_params=_cparams(("parallel", "arbitrary", "arbitrary"), VMEM_LIMIT_LARGE),
        name="proj_residual",
    )(a, b, res, next_gain.reshape(1, n).astype(F32))


def _merge_kernel(u_ref, r_ref, oh_ref, oa_ref, wgh_ref, wga_ref, wbh_ref, wba_ref, o_ref):
    u = u_ref[...]
    r = r_ref[...]
    oh = jnp.concatenate([oh_ref[h] for h in range(oh_ref.shape[0])], axis=1)
    gh = jnp.dot(u, wgh_ref[...], preferred_element_type=F32) * r
    ga = jnp.dot(u, wga_ref[...], preferred_element_type=F32) * r
    bh = jnp.dot(oh, wbh_ref[...], preferred_element_type=F32)
    ba = jnp.dot(oa_ref[...], wba_ref[...], preferred_element_type=F32)
    o_ref[...] = (_sigmoid(gh) * bh + _sigmoid(ga) * ba).astype(o_ref.dtype)


def _merge(u, r, oh, oa, wgh, wga, wbh, wba):
    m, d = u.shape
    nh = oh.shape[0]
    n = wgh.shape[1]
    tm = _pick(m, ROW_TILES)
    tn = _pick(n, (256, 128))
    return pl.pallas_call(
        _merge_kernel,
        grid=(m // tm, n // tn),
        in_specs=[pl.BlockSpec((tm, d), lambda i, j: (i, 0)),
                  pl.BlockSpec((tm, 1), lambda i, j: (i, 0)),
                  pl.BlockSpec((nh, tm, LANES), lambda i, j: (0, i, 0)),
                  pl.BlockSpec((tm, oa.shape[1]), lambda i, j: (i, 0)),
                  pl.BlockSpec((d, tn), lambda i, j: (0, j)),
                  pl.BlockSpec((d, tn), lambda i, j: (0, j)),
                  pl.BlockSpec((wbh.shape[0], tn), lambda i, j: (0, j)),
                  pl.BlockSpec((wba.shape[0], tn), lambda i, j: (0, j))],
        out_specs=pl.BlockSpec((tm, tn), lambda i, j: (i, j)),
        out_shape=jax.ShapeDtypeStruct((m, n), BF16),
        compiler_params=_cparams(("parallel", "arbitrary")),
        name="branch_merge",
    )(u, r, oh, oa, wgh, wga, wbh, wba)


def _rope_tail(y2, tab):
    w = y2 * tab
    rot = w + pltpu.roll(w, QK_ROPE, axis=1)
    lane = lax.broadcasted_iota(jnp.int32, rot.shape, 1)
    return jnp.where(lane < QK_ROPE, rot, 0.0)


def _rope_sumsq(y2):
    lane = lax.broadcasted_iota(jnp.int32, y2.shape, 1)
    y2m = jnp.where(lane < QK_ROPE, y2, 0.0)
    return jnp.sum(y2m * y2m, axis=-1, keepdims=True)


def _head_rms_scale(y, ones_mask, extra_ss):
    ss = jnp.dot((y * y).astype(BF16), ones_mask, preferred_element_type=F32)
    return lax.rsqrt((ss + extra_ss) * (1.0 / QK_HEAD) + RMS_EPS)


def _latent_norm(x, w):
    ms = jnp.mean(x * x, axis=-1, keepdims=True)
    return ((x * lax.rsqrt(ms + RMS_EPS)) * w).astype(BF16)


def _qkv_kernel(lat_ref, qlw_ref, kvlw_ref, wq_ref, wkv_ref, gq_ref, gk_ref, tab_ref,
                q_ref, k_ref, v_ref, *, q_lora, kv_lora, q_scale):
    xq = _latent_norm(lat_ref[:, 0:q_lora], qlw_ref[...])
    xkv = _latent_norm(lat_ref[:, q_lora:q_lora + kv_lora], kvlw_ref[...])
    kr = lat_ref[:, q_lora + kv_lora:q_lora + kv_lora + LANES]
    gq1 = gq_ref[:, :QK_NOPE] * q_scale
    gk1 = gk_ref[:, :QK_NOPE]
    tab_q = tab_ref[...] * (gq_ref[:, QK_NOPE:] * q_scale)
    tab_k = tab_ref[...] * gk_ref[:, QK_NOPE:]
    kr_ss = _rope_sumsq(kr)
    row = lax.broadcasted_iota(jnp.int32, (QK_PAD, LANES), 0)
    ones_q = jnp.where(row < QK_HEAD, 1.0, 0.0).astype(BF16)
    ones_k = jnp.ones((QK_NOPE, LANES), BF16)

    def body(h, carry):
        y = jnp.dot(xq, wq_ref[h], preferred_element_type=F32)
        y1, y2 = y[:, :QK_NOPE], y[:, QK_NOPE:]
        r = _head_rms_scale(y, ones_q, 0.0)
        q_ref[h, :, 0:QK_NOPE] = ((y1 * r) * gq1).astype(BF16)
        q_ref[h, :, QK_NOPE:] = _rope_tail(y2 * r, tab_q).astype(BF16)

        z = jnp.dot(xkv, wkv_ref[h], preferred_element_type=F32)
        z1 = z[:, :QK_NOPE]
        rk = _head_rms_scale(z1, ones_k, kr_ss)
        k_ref[h, :, 0:QK_NOPE] = ((z1 * rk) * gk1).astype(BF16)
        k_ref[h, :, QK_NOPE:] = _rope_tail(kr * rk, tab_k).astype(BF16)
        v_ref[h] = z[:, QK_NOPE:].astype(BF16)
        return carry

    lax.fori_loop(0, q_ref.shape[0], body, 0, unroll=2)


def _qkv_proj(lat, q_lat_w, kv_lat_w, wq, wkv, gq, gk, tab, row_off, rows, q_lora, kv_lora):
    nh = wq.shape[0]
    tl = _pick(math.gcd(rows, row_off) if row_off else rows, (512, 256, 128, 64, 32, 16))
    off = row_off // tl
    full = lambda shape: pl.BlockSpec(shape, lambda i: (0,) * len(shape))
    return pl.pallas_call(
        functools.partial(_qkv_kernel, q_lora=q_lora, kv_lora=kv_lora,
                          q_scale=LOG2E / math.sqrt(QK_HEAD)),
        grid=(rows // tl,),
        in_specs=[pl.BlockSpec((tl, lat.shape[1]), lambda i: (off + i, 0)),
                  full((1, q_lora)), full((1, kv_lora)),
                  full((nh, q_lora, QK_PAD)), full((nh, kv_lora, QK_NOPE + V_HEAD)),
                  full((1, QK_PAD)), full((1, QK_PAD)),
                  pl.BlockSpec((tl, LANES), lambda i: (i, 0))],
        out_specs=[pl.BlockSpec((nh, tl, QK_PAD), lambda i: (0, i, 0)),
                   pl.BlockSpec((nh, tl, QK_PAD), lambda i: (0, i, 0)),
                   pl.BlockSpec((nh, tl, V_HEAD), lambda i: (0, i, 0))],
        out_shape=[jax.ShapeDtypeStruct((nh, rows, QK_PAD), BF16),
                   jax.ShapeDtypeStruct((nh, rows, QK_PAD), BF16),
                   jax.ShapeDtypeStruct((nh, rows, V_HEAD), BF16)],
        compiler_params=_cparams(("parallel",)),
        name="mla_qkv_proj",
    )(lat, q_lat_w, kv_lat_w, wq, wkv, gq, gk, tab)


def _attn_kernel(q_ref, k_ref, v_ref, o_ref, *, n_valid):
    q = q_ref[0]
    lp = k_ref.shape[1]
    kb = max(QK_PAD, 1 << ((lp // 2).bit_length() - 1))
    edges = list(range(0, min(lp, n_valid), kb)) + [min(lp, -(-n_valid // LANES) * LANES)]
    m = denom = acc = None
    for c0, c1 in zip(edges[:-1], edges[1:]):
        s = lax.dot_general(q, k_ref[0, c0:c1, :], (((1,), (1,)), ((), ())), preferred_element_type=F32)
        if c1 > n_valid:
            col = lax.broadcasted_iota(jnp.int32, s.shape, 1) + c0
            s = jnp.where(col < n_valid, s, NEG_BIG)
        mb = jnp.max(s, axis=-1, keepdims=True)
        if m is None:
            m = mb
            p = jnp.exp2(s - m)
            denom = jnp.sum(p, axis=-1, keepdims=True)
            acc = jnp.dot(p.astype(BF16), v_ref[0, c0:c1, :], preferred_element_type=F32)
        else:
            m_new = jnp.maximum(m, mb)
            alpha = jnp.exp2(m - m_new)
            p = jnp.exp2(s - m_new)
            denom = alpha * denom + jnp.sum(p, axis=-1, keepdims=True)
            acc = alpha * acc + jnp.dot(p.astype(BF16), v_ref[0, c0:c1, :], preferred_element_type=F32)
            m = m_new
    o_ref[...] = (acc / denom).astype(o_ref.dtype)


def _attention(q, k, v, batch, lp, n_valid):
    nh = q.shape[0]
    nq = next(n for n in range(1, lp + 1) if lp % n == 0 and (lp // n) % 16 == 0 and lp // n <= ATTN_MAX_Q_ROWS)
    tq = lp // nq
    return pl.pallas_call(
        functools.partial(_attn_kernel, n_valid=n_valid),
        grid=(batch, nh, nq),
        in_specs=[pl.BlockSpec((1, tq, QK_PAD), lambda b, h, i: (h, b * nq + i, 0)),
                  pl.BlockSpec((1, lp, QK_PAD), lambda b, h, i: (h, b, 0)),
                  pl.BlockSpec((1, lp, V_HEAD), lambda b, h, i: (h, b, 0))],
        out_specs=pl.BlockSpec((tq, V_HEAD), lambda b, h, i: (b * nq + i, h)),
        out_shape=jax.ShapeDtypeStruct((batch * lp, nh * V_HEAD), BF16),
        compiler_params=_cparams(("parallel", "parallel", "arbitrary")),
        name="mla_attention",
    )(q, k, v)


def _lower_bound(logit_ref, h, layer):
    lg = logit_ref[0, h]
    e = jnp.exp(lg - jnp.max(lg, axis=0, keepdims=True))
    p = e / jnp.sum(e, axis=0, keepdims=True)
    lb = jnp.zeros((1, lg.shape[1]), F32)
    for j in range(1, layer + 1):
        lb = lb + p[j:j + 1]
    return lb


def _gates(zq, zf, zi, lb, row_ok):
    q = zq * _sigmoid(zq)
    e = jnp.exp(-jnp.abs(zf))
    r = 1.0 / (1.0 + e)
    er = e * r
    pos = zf >= 0.0
    k = jnp.where(row_ok, (1.0 - lb) * jnp.where(pos, er, r), 0.0)
    f = lb + (1.0 - lb) * jnp.where(pos, r, er)
    g = jnp.maximum(jnp.log(f) * LOG2E, (jnp.log1p(-lb) * LOG2E - 1.0) + jnp.minimum(zf, 0.0) * LOG2E)
    return q, k, zi, g


def _log_cumsum(g, tri):
    g1 = g.astype(BF16)
    r1 = g - g1.astype(F32)
    g2 = r1.astype(BF16)
    g3 = (r1 - g2.astype(F32)).astype(BF16)
    bb = jnp.dot(tri, jnp.concatenate([g1, g2, g3], axis=1), preferred_element_type=F32)
    return (bb[:, :LANES] + bb[:, LANES:2 * LANES]) + bb[:, 2 * LANES:]


def _carried(q, k, v, b, total, st):
    qh = (q * jnp.exp2(b)).astype(BF16)
    kh = (k * jnp.exp2(total - b)).astype(BF16)
    o = lax.dot_general(qh, st.astype(BF16), (((1,), (1,)), ((), ())), preferred_element_type=F32)
    st_new = st * jnp.exp2(total) + lax.dot_general(
        v.astype(BF16), kh, (((0,), (0,)), ((), ())), preferred_element_type=F32)
    return o, st_new


def _hgrn_chunk_factorised(q, k, v, g, st, tri, in_scan, reverse):
    c_rows = q.shape[0]
    sub = HG_SUB_ROWS
    b = _log_cumsum(g, tri)
    total = b[0:1] if reverse else b[c_rows - 1:c_rows]
    o, st_new = _carried(q, k, v, b, total, st)

    zero_row = jnp.zeros((1, LANES), F32)
    growth = zero_row
    rows = []
    for blk in range(c_rows // sub):
        lo, hi = blk * sub, (blk + 1) * sub
        if reverse:
            ref = b[hi:hi + 1] if hi < c_rows else zero_row
            growth = jnp.maximum(growth, ref - b[lo:lo + 1])
        else:
            ref = b[lo - 1:lo] if lo > 0 else zero_row
            growth = jnp.maximum(growth, ref - b[hi - 1:hi])
        qi = (q[lo:hi] * jnp.exp2(b[lo:hi] - ref)).astype(BF16)
        ks = (k * jnp.exp2(jnp.minimum(ref - b, HG_MAX_LOG2_GROWTH))).astype(BF16)
        rows.append(lax.dot_general(qi, ks, (((1,), (1,)), ((), ())), preferred_element_type=F32))
    a = jnp.where(in_scan, jnp.concatenate(rows, axis=0), 0.0)
    o = o + jnp.dot(a.astype(BF16), v.astype(BF16), preferred_element_type=F32)
    return o, st_new, growth


def _hgrn_chunk_exact(q, k, v, g, st, tri, ones_sq, reverse):
    c_rows = q.shape[0]
    sub = HG_EXACT_SUB_ROWS
    nb = c_rows // sub
    b = _log_cumsum(g, tri)
    total = b[0:1] if reverse else b[c_rows - 1:c_rows]
    o, _ = _carried(q, k, v, b, total, st)

    vb = v.astype(BF16)
    pieces = []
    for blk in range(nb):
        lo, hi = blk * sub, (blk + 1) * sub
        if reverse:
            if blk == nb - 1:
                pieces.append(jnp.zeros((sub, LANES), F32))
                continue
            ref = b[hi:hi + 1]
            other = slice(hi, c_rows)
        else:
            if blk == 0:
                pieces.append(jnp.zeros((sub, LANES), F32))
                continue
            ref = b[lo - 1:lo]
            other = slice(0, lo)
        qi = (q[lo:hi] * jnp.exp2(b[lo:hi] - ref)).astype(BF16)
        ko = (k[other] * jnp.exp2(ref - b[other])).astype(BF16)
        a = lax.dot_general(qi, ko, (((1,), (1,)), ((), ())), preferred_element_type=F32)
        pieces.append(jnp.dot(a.astype(BF16), vb[other], preferred_element_type=F32))
    o = o + jnp.concatenate(pieces, axis=0)

    pos = lax.broadcasted_iota(jnp.int32, (c_rows, LANES), 0) % sub
    d_list, v_list = [], []
    for d in range(sub):
        if d == 0:
            vs = v
            dmat = q * k
        else:
            shift = (c_rows - d) if reverse else d
            ks = pltpu.roll(k, shift, axis=0)
            bs = pltpu.roll(b, shift, axis=0)
            vs = pltpu.roll(v, shift, axis=0)
            ok = (pos + d < sub) if reverse else (pos >= d)
            dmat = (q * ks) * jnp.exp2(jnp.where(ok, b - bs, NEG_BIG))
        d_list.append(dmat.astype(BF16))
        v_list.append(vs)
    rs = jnp.dot(jnp.concatenate(d_list, axis=0), ones_sq, preferred_element_type=F32)
    for d in range(sub):
        o = o + rs[d * c_rows:(d + 1) * c_rows] * v_list[d]
    return o


def _hgrn_kernel(info_ref, *refs, layer, reverse):
    if reverse:
        zq_ref, zf_ref, zi_ref, zg_ref, of_ref, lg_ref, nw_ref, o_ref, st_ref = refs
    else:
        zq_ref, zf_ref, zi_ref, lg_ref, o_ref, st_ref = refs
    step = pl.program_id(0)
    g = (pl.num_programs(0) - 1 - step) if reverse else step
    local, n_chunks, seq_rows = info_ref[3 * g], info_ref[3 * g + 1], info_ref[3 * g + 2]
    nh, c_rows = zq_ref.shape[0], zq_ref.shape[1]
    cur = step % 2

    @pl.when((local == n_chunks - 1) if reverse else (local == 0))
    def _():
        st_ref[cur] = jnp.zeros(st_ref.shape[1:], F32)

    rr = lax.broadcasted_iota(jnp.int32, (c_rows, c_rows), 0)
    cc = lax.broadcasted_iota(jnp.int32, (c_rows, c_rows), 1)
    in_scan = (cc >= rr) if reverse else (cc <= rr)
    tri = jnp.where(in_scan, 1.0, 0.0).astype(BF16)
    row_ok = (lax.broadcasted_iota(jnp.int32, (c_rows, LANES), 0) + local * c_rows) < seq_rows

    def finish(h, o):
        if reverse:
            o = o + of_ref[h]
            ms = jnp.mean(o * o, axis=-1, keepdims=True)
            zg = zg_ref[h]
            o_ref[h] = (((o * lax.rsqrt(ms + RMS_EPS)) * nw_ref[...]) * (zg * _sigmoid(zg))).astype(o_ref.dtype)
        else:
            o_ref[h] = o

    def fast_body(h, growth):
        q, k, v, gl = _gates(zq_ref[h], zf_ref[h], zi_ref[h], _lower_bound(lg_ref, h, layer), row_ok)
        o, st_new, gr = _hgrn_chunk_factorised(q, k, v, gl, st_ref[cur, h], tri, in_scan, reverse)
        st_ref[1 - cur, h] = st_new
        finish(h, o)
        return jnp.maximum(growth, gr)

    growth = lax.fori_loop(0, nh, fast_body, jnp.zeros((1, LANES), F32), unroll=16)

    @pl.when(jnp.max(growth) > HG_MAX_LOG2_GROWTH)
    def _():
        ones_sq = jnp.ones((LANES, LANES), BF16)

        def exact_body(h, carry):
            q, k, v, gl = _gates(zq_ref[h], zf_ref[h], zi_ref[h], _lower_bound(lg_ref, h, layer), row_ok)
            finish(h, _hgrn_chunk_exact(q, k, v, gl, st_ref[cur, h], tri, ones_sq, reverse))
            return carry

        lax.fori_loop(0, nh, exact_body, 0)


def _hgrn(hg, info, lb_logits, out_norm, layer, nh):
    m = hg.shape[1]
    c = HG_CHUNK_ROWS
    n_chunks = m // c
    depth = lb_logits.shape[1]
    lg = lb_logits.astype(F32).reshape(2, depth, nh, LANES).transpose(0, 2, 1, 3)

    def blk(grp, rev):
        if rev:
            return pl.BlockSpec((nh, c, LANES), lambda g, info_ref: (grp, n_chunks - 1 - g, 0))
        return pl.BlockSpec((nh, c, LANES), lambda g, info_ref: (grp, g, 0))

    state = pltpu.VMEM((2, nh, LANES, LANES), F32)
    o_f = pl.pallas_call(
        functools.partial(_hgrn_kernel, layer=layer, reverse=False),
        grid_spec=pltpu.PrefetchScalarGridSpec(
            num_scalar_prefetch=1, grid=(n_chunks,),
            in_specs=[blk(0, False), blk(1, False), blk(3, False),
                      pl.BlockSpec((1, nh, depth, LANES), lambda g, info_ref: (0, 0, 0, 0))],
            out_specs=blk(0, False),
            scratch_shapes=[state]),
        out_shape=jax.ShapeDtypeStruct((nh, m, LANES), F32),
        compiler_params=_cparams(("arbitrary",)),
        name="hgrn_forward",
    )(info, hg, hg, hg, lg)

    return pl.pallas_call(
        functools.partial(_hgrn_kernel, layer=layer, reverse=True),
        grid_spec=pltpu.PrefetchScalarGridSpec(
            num_scalar_prefetch=1, grid=(n_chunks,),
            in_specs=[blk(0, True), blk(2, True), blk(3, True), blk(4, True), blk(0, True),
                      pl.BlockSpec((1, nh, depth, LANES), lambda g, info_ref: (1, 0, 0, 0)),
                      pl.BlockSpec((1, LANES), lambda g, info_ref: (0, 0))],
            out_specs=blk(0, True),
            scratch_shapes=[state]),
        out_shape=jax.ShapeDtypeStruct((nh, m, LANES), BF16),
        compiler_params=_cparams(("arbitrary",)),
        name="hgrn_backward",
    )(info, hg, hg, hg, hg, o_f, lg, out_norm.reshape(1, LANES).astype(F32))


def _swap_halves(x, axis):
    a, b = jnp.split(x, 2, axis=axis)
    return jnp.concatenate([b, a], axis=axis)


def _rope_table(rows):
    inv_freq = 1.0 / (ROPE_THETA ** (jnp.arange(0, QK_ROPE, 2, dtype=F32) / QK_ROPE))
    ang = jnp.arange(rows, dtype=F32)[:, None] * inv_freq[None, :]
    c, s = jnp.cos(ang), jnp.sin(ang)
    return jnp.concatenate([c, c, -s, s], axis=1)


def _head_gain(g):
    g = g.astype(F32)
    return jnp.concatenate([g[:QK_NOPE], g[QK_NOPE:], _swap_halves(g[QK_NOPE:], 0)]).reshape(1, QK_PAD)


def _ffn_weights(w_in, w_out):
    d_ff = w_out.shape[0]
    return w_in[:, :d_ff].astype(BF16), w_in[:, d_ff:].astype(BF16), w_out.astype(BF16)


def _ffn(h, hb, r, weights, next_gain, scale=0.5):
    wg, wu, wo = weights
    return _matmul_residual(_ffn_in(hb, r, wg, wu), wo, h, scale, next_gain)


def kernel(x_prompt, x_sample, meta_tokens, hgrn_lb_logits, norm_ffn1, w_ffn1_in, w_ffn1_out, norm_mix, w_in,
           q_lat_norm, w_uq, kv_lat_norm, w_ukv, q_head_norm, k_head_norm, hg_out_norm,
           w_branch_hgrn, w_branch_mla, w_out, norm_ffn2, w_ffn2_in, w_ffn2_out):
    depth = norm_ffn1.shape[0]
    d_model = x_prompt.shape[-1]
    hg_dim = hgrn_lb_logits.shape[-1]
    hg_heads = hg_dim // HG_HEAD_DIM
    q_lora = q_lat_norm.shape[-1]
    kv_lora = kv_lat_norm.shape[-1]
    mla_heads = w_uq.shape[-1] // QK_HEAD

    trunks = []
    row_off = 0
    blocks, info = [], []
    meta = meta_tokens.astype(F32)
    pad = jnp.zeros((SEQ_PAD - N_META, d_model), F32)
    for x in (x_prompt, x_sample):
        bsz, s, _ = x.shape
        lp = s + SEQ_PAD
        for b in range(bsz):
            blocks += [meta, x[b].astype(F32), pad]
        nc = lp // HG_CHUNK_ROWS
        for _ in range(bsz):
            for c in range(nc):
                info += [c, nc, s + N_META]
        trunks.append((bsz, s, lp, row_off))
        row_off += bsz * lp
    h = jnp.concatenate(blocks, axis=0)
    hb, r = _row_scale(h, norm_ffn1[0].reshape(1, d_model).astype(F32))
    info = jnp.asarray(info, jnp.int32)
    tabs = [jnp.tile(_rope_table(lp), (bsz, 1)) for (bsz, s, lp, _) in trunks]

    sizes = (hg_dim,) * 5 + (q_lora, kv_lora, QK_ROPE, d_model, d_model)
    offs = [0]
    for sz in sizes:
        offs.append(offs[-1] + sz)

    for l in range(depth):
        ffn1 = _ffn_weights(w_ffn1_in[l], w_ffn1_out[l])
        ffn2 = _ffn_weights(w_ffn2_in[l], w_ffn2_out[l])
        wl = w_in[l]
        w_hg = wl[:, :offs[5]].astype(BF16)
        w_kr = wl[:, offs[7]:offs[8]]
        w_lat = jnp.concatenate([wl[:, offs[5]:offs[7]], w_kr, _swap_halves(w_kr, 1)], axis=1).astype(BF16)
        w_gh = wl[:, offs[8]:offs[9]].astype(BF16)
        w_ga = wl[:, offs[9]:offs[10]].astype(BF16)
        after_ffn2 = norm_ffn1[l + 1] if l + 1 < depth else jnp.ones((d_model,), F32)
        wq = w_uq[l].reshape(q_lora, mla_heads, QK_HEAD)
        wq = jnp.concatenate([wq, _swap_halves(wq[:, :, QK_NOPE:], 2)], axis=2)
        wq = wq.transpose(1, 0, 2).astype(BF16)
        wkv = w_ukv[l].reshape(kv_lora, mla_heads, QK_NOPE + V_HEAD).transpose(1, 0, 2).astype(BF16)

        h, hb, r = _ffn(h, hb, r, ffn1, norm_mix[l])

        hg = _matmul(hb, r, w_hg, head_major=True)
        lat = _matmul(hb, r, w_lat)
        o_h = _hgrn(hg, info, hgrn_lb_logits, hg_out_norm[l], l, hg_heads)

        o_a = []
        for (bsz, s, lp, off), tab in zip(trunks, tabs):
            q, k, v = _qkv_proj(lat, q_lat_norm[l].reshape(1, q_lora).astype(F32),
                                kv_lat_norm[l].reshape(1, kv_lora).astype(F32), wq, wkv,
                                _head_gain(q_head_norm[l]), _head_gain(k_head_norm[l]),
                                tab, off, bsz * lp, q_lora, kv_lora)
            o_a.append(_attention(q, k, v, bsz, lp, s + N_META))
        o_a = jnp.concatenate(o_a, axis=0)

        merged = _merge(hb, r, o_h, o_a, w_gh, w_ga, w_branch_hgrn[l].astype(BF16), w_branch_mla[l].astype(BF16))
        h, hb, r = _matmul_residual(merged, w_out[l].astype(BF16), h, 1.0, norm_ffn2[l])

        h, hb, r = _ffn(h, hb, r, ffn2, after_ffn2)

    outs = []
    for (bsz, s, lp, off) in trunks:
        outs.append(h[off:off + bsz * lp].reshape(bsz, lp, d_model)[:, N_META:N_META + s])
    return tuple(outs)
```
